```python
import jax, jax.numpy as jnp
from jax import lax
import numpy as np

D_MODEL = 1024
BATCH = 4
SEQ = 4096
DEPTH = 1

CHUNK = 64
ATT_HEADS = 8
HEAD_DIM = 64
ATT_WIDTH = ATT_HEADS * HEAD_DIM
LRU_WIDTH = D_MODEL - ATT_WIDTH
LRU_BLOCKS = 8
LRU_BLOCK_DIM = LRU_WIDTH // LRU_BLOCKS
CONV_WIDTH = 4
LRU_C = 8.0
Q_BLOCK = 128
N_GROUPS = 4
EXPERTS_PER_GROUP = 8
TOP_K_INNER = 2
D_EXPERT = 256
RMS_EPS = 1e-6
IN_SIZES = (ATT_WIDTH, ATT_WIDTH, ATT_WIDTH, ATT_HEADS, LRU_WIDTH, LRU_WIDTH)
IN_COLS = sum(IN_SIZES)

kernel_name = 'hymba_fox_rglru_hmoe_block'


def rmsnorm(x, g):
    xf = x.astype(jnp.float32)
    inv = lax.rsqrt(jnp.mean(xf * xf, axis=-1, keepdims=True) + RMS_EPS)
    return (xf * inv * g.astype(jnp.float32)).astype(x.dtype)


def forgetting_attention(q, k, v, log_f):
    b, s, h, dh = q.shape
    c = jnp.cumsum(log_f, axis=1).transpose(0, 2, 1)
    n_blk = s // Q_BLOCK
    q_blk = q.reshape(b, n_blk, Q_BLOCK, h, dh).transpose(1, 0, 2, 3, 4)
    c_blk = c.reshape(b, h, n_blk, Q_BLOCK).transpose(2, 0, 1, 3)
    starts = jnp.arange(n_blk, dtype=jnp.int32) * Q_BLOCK
    k_pos = jnp.arange(s, dtype=jnp.int32)
    scale = HEAD_DIM ** -0.5

    def one_block(args):
        qb, cb, start = args
        logits = jnp.einsum('bqhd,bkhd->bhqk', qb, k, preferred_element_type=jnp.float32) * scale
        decay = cb[..., :, None] - c[..., None, :]
        q_pos = start + jnp.arange(Q_BLOCK, dtype=jnp.int32)
        causal = k_pos[None, :] <= q_pos[:, None]
        logits = jnp.where(causal, logits + decay, -1e30)
        p = jax.nn.softmax(logits, axis=-1)
        return jnp.einsum('bhqk,bkhd->bqhd', p.astype(v.dtype), v)

    out = lax.map(one_block, (q_blk, c_blk, starts))
    return out.transpose(1, 0, 2, 3, 4).reshape(b, s, h * dh)


def causal_depthwise_conv(x, w, bias):
    s = x.shape[1]
    xp = jnp.pad(x, ((0, 0), (CONV_WIDTH - 1, 0), (0, 0)))
    out = bias + w[0] * xp[:, 0:s]
    for tap in range(1, CONV_WIDTH):
        out = out + w[tap] * xp[:, tap:tap + s]
    return out


def block_diag_linear(x, w, b):
    xb = x.reshape(x.shape[:-1] + (LRU_BLOCKS, LRU_BLOCK_DIM))
    y = jnp.einsum('bsni,nij->bsnj', xb, w) + b
    return y.reshape(x.shape)


def rg_lru(x, w_a, b_a, w_x, b_x, lam):
    r = jax.nn.sigmoid(block_diag_linear(x, w_a, b_a)).astype(jnp.float32)
    i = jax.nn.sigmoid(block_diag_linear(x, w_x, b_x))
    log_a = -LRU_C * r * jax.nn.softplus(-lam.astype(jnp.float32))
    a = jnp.exp(log_a)
    mult = jnp.sqrt(-jnp.expm1(2.0 * log_a))
    u = mult * (i * x).astype(jnp.float32)

    def combine(left, right):
        a_l, u_l = left
        a_r, u_r = right
        return a_r * a_l, a_r * u_l + u_r

    _, hseq = lax.associative_scan(combine, (a, u), axis=1)
    return hseq.astype(x.dtype)


def hierarchical_moe(t, w_group, b_group, w_inner, b_inner, w_gate, w_up, w_down):
    group_p = jax.nn.softmax((t @ w_group + b_group).astype(jnp.float32), axis=-1)
    g_w, g_idx = lax.top_k(group_p, 1)
    inner_all = jnp.einsum('nd,gde->nge', t, w_inner) + b_inner
    inner_logits = jnp.take_along_axis(inner_all, g_idx[:, :, None], axis=1)[:, 0]
    inner_p = jax.nn.softmax(inner_logits.astype(jnp.float32), axis=-1)
    e_w, e_idx = lax.top_k(inner_p, TOP_K_INNER)
    e_w = e_w / jnp.sum(e_w, axis=-1, keepdims=True)
    expert_w = jnp.sum(jax.nn.one_hot(e_idx, EXPERTS_PER_GROUP, dtype=jnp.float32) * e_w[..., None], axis=1)
    combine = (g_w[:, :, None]
               * jax.nn.one_hot(g_idx[:, 0], N_GROUPS, dtype=jnp.float32)[:, :, None]
               * expert_w[:, None, :]).astype(t.dtype)
    y = jnp.zeros_like(t)
    for g in range(N_GROUPS):
        hg = jax.nn.silu(jnp.einsum('nd,edf->nef', t, w_gate[g])) * jnp.einsum('nd,edf->nef', t, w_up[g])
        y = y + jnp.einsum('nef,efd->nd', hg * combine[:, g, :, None], w_down[g])
    return y


def setup_inputs(seed: int = 0) -> dict:
    key = jax.random.key(seed)
    ks = jax.random.split(key, 24)
    nrm = jax.random.normal
    L, D, G, E, F = DEPTH, D_MODEL, N_GROUPS, EXPERTS_PER_GROUP, D_EXPERT
    a0 = jax.random.uniform(ks[10], (L, LRU_WIDTH), minval=0.9, maxval=0.999)
    sig = a0 ** (1.0 / LRU_C)
    lam = jnp.log(sig) - jnp.log1p(-sig)
    return {
        'x': nrm(ks[0], (BATCH, SEQ, D), jnp.float32),
        'mix_norm': 1.0 + 0.05 * nrm(ks[1], (L, D), jnp.float32),
        'w_in': nrm(ks[2], (L, D, IN_COLS), jnp.float32) * D ** -0.5,
        'b_forget': jnp.linspace(1.0, 6.0, ATT_HEADS, dtype=jnp.float32) + 0.1 * nrm(ks[3], (L, ATT_HEADS), jnp.float32),
        'conv_w': nrm(ks[4], (L, CONV_WIDTH, LRU_WIDTH), jnp.float32) * CONV_WIDTH ** -0.5,
        'conv_b': 0.02 * nrm(ks[5], (L, LRU_WIDTH), jnp.float32),
        'w_a': nrm(ks[6], (L, LRU_BLOCKS, LRU_BLOCK_DIM, LRU_BLOCK_DIM), jnp.float32) * LRU_BLOCK_DIM ** -0.5,
        'b_a': 0.02 * nrm(ks[7], (L, LRU_BLOCKS, LRU_BLOCK_DIM), jnp.float32),
        'w_x': nrm(ks[8], (L, LRU_BLOCKS, LRU_BLOCK_DIM, LRU_BLOCK_DIM), jnp.float32) * LRU_BLOCK_DIM ** -0.5,
        'b_x': 0.02 * nrm(ks[9], (L, LRU_BLOCKS, LRU_BLOCK_DIM), jnp.float32),
        'lru_lambda': lam,
        'w_out': nrm(ks[11], (L, D, D), jnp.float32) * D ** -0.5,
        'ffn_norm': 1.0 + 0.05 * nrm(ks[12], (L, D), jnp.float32),
        'w_group': nrm(ks[13], (L, D, G), jnp.float32) * D ** -0.5,
        'b_group': 0.01 * nrm(ks[14], (L, G), jnp.float32),
        'w_inner': nrm(ks[15], (L, G, D, E), jnp.float32) * D ** -0.5,
        'b_inner': 0.01 * nrm(ks[16], (L, G, E), jnp.float32),
        'w_gate': nrm(ks[17], (L, G, E, D, F), jnp.float32) * D ** -0.5,
        'w_up': nrm(ks[18], (L, G, E, D, F), jnp.float32) * D ** -0.5,
        'w_down': nrm(ks[19], (L, G, E, F, D), jnp.float32) * F ** -0.5,
        'final_norm': 1.0 + 0.05 * nrm(ks[20], (D,), jnp.float32),
    }


def reference(x, mix_norm, w_in, b_forget, conv_w, conv_b, w_a, b_a, w_x, b_x, lru_lambda,
              w_out, ffn_norm, w_group, b_group, w_inner, b_inner, w_gate, w_up, w_down, final_norm):
    b, s, d = x.shape
    split_points = [int(v) for v in np.cumsum(IN_SIZES)[:-1]]
    for l in range(DEPTH):
        h = rmsnorm(x, mix_norm[l])
        z = h @ w_in[l]
        q, k, v, f_logit, gate_in, rec_in = jnp.split(z, split_points, axis=-1)
        log_f = jax.nn.log_sigmoid(f_logit.astype(jnp.float32) + b_forget[l].astype(jnp.float32))
        att = forgetting_attention(q.reshape(b, s, ATT_HEADS, HEAD_DIM),
                                   k.reshape(b, s, ATT_HEADS, HEAD_DIM),
                                   v.reshape(b, s, ATT_HEADS, HEAD_DIM), log_f)
        rec = causal_depthwise_conv(rec_in, conv_w[l], conv_b[l])
        rec = rg_lru(rec, w_a[l], b_a[l], w_x[l], b_x[l], lru_lambda[l])
        lru_out = rec * jax.nn.gelu(gate_in, approximate=True)
        x = x + jnp.concatenate([att, lru_out], axis=-1) @ w_out[l]
        h2 = rmsnorm(x, ffn_norm[l]).reshape(b * s, d)
        y = hierarchical_moe(h2, w_group[l], b_group[l], w_inner[l], b_inner[l],
                             w_gate[l], w_up[l], w_down[l])
        x = x + y.reshape(b, s, d)
    return rmsnorm(x, final_norm)
```

```python
import functools

import jax
import jax.numpy as jnp
from jax import lax
from jax.experimental import pallas as pl
from jax.experimental.pallas import tpu as pltpu

ATT_HEADS = 8
HEAD_DIM = 64
ATT_WIDTH = ATT_HEADS * HEAD_DIM
LRU_WIDTH = 512
LRU_BLOCKS = 8
CONV_WIDTH = 4
LRU_C = 8.0
N_GROUPS = 4
EXPERTS_PER_GROUP = 8
N_EXPERTS = N_GROUPS * EXPERTS_PER_GROUP
D_EXPERT = 256
RMS_EPS = 1e-6
LANES = 128
SUBLANES = 8
MASK_VALUE = -1e30
VMEM_LIMIT = 48 * 1024 * 1024

BF16 = jnp.bfloat16
F32 = jnp.float32
HI16 = -65536


def _rms(x, g):
    inv = lax.rsqrt(jnp.mean(x * x, axis=-1, keepdims=True) + RMS_EPS)
    return x * inv * g


def _pack_pair(lo, hi):
    lo_b = lax.bitcast_convert_type(lo.astype(BF16).astype(F32), jnp.int32)
    hi_b = lax.bitcast_convert_type(hi.astype(BF16).astype(F32), jnp.int32)
    return (hi_b & HI16) | lax.shift_right_logical(lo_b, 16)


def _unpack_pair(p):
    lo = lax.bitcast_convert_type(lax.shift_left(p, 16), F32)
    hi = lax.bitcast_convert_type(p & HI16, F32)
    return lo, hi


def _in_proj_kernel(x_ref, g_ref, w_ref, qkv_ref, gr_ref, f_ref):
    h = _rms(x_ref[...], g_ref[...]).astype(BF16)
    nq = qkv_ref.shape[1]
    ng = gr_ref.shape[1]
    qkv_ref[...] = jnp.dot(h, w_ref[:, :nq], preferred_element_type=F32).astype(BF16)
    gr_ref[...] = jnp.dot(h, w_ref[:, nq:nq + ng], preferred_element_type=F32)
    f_ref[...] = jnp.dot(h, w_ref[:, nq + ng:], preferred_element_type=F32)


def _in_proj(x2, g, w, tm):
    n, d = x2.shape
    nq, ng = 3 * ATT_WIDTH, 2 * LRU_WIDTH
    return pl.pallas_call(
        _in_proj_kernel,
        out_shape=(jax.ShapeDtypeStruct((n, nq), BF16),
                   jax.ShapeDtypeStruct((n, ng), F32),
                   jax.ShapeDtypeStruct((n, LANES), F32)),
        grid=(n // tm,),
        in_specs=[pl.BlockSpec((tm, d), lambda i: (i, 0)),
                  pl.BlockSpec((1, d), lambda i: (0, 0)),
                  pl.BlockSpec(w.shape, lambda i: (0, 0))],
        out_specs=(pl.BlockSpec((tm, nq), lambda i: (i, 0)),
                   pl.BlockSpec((tm, ng), lambda i: (i, 0)),
                   pl.BlockSpec((tm, LANES), lambda i: (i, 0))),
        compiler_params=pltpu.CompilerParams(
            dimension_semantics=("arbitrary",), vmem_limit_bytes=VMEM_LIMIT),
        name="in_proj",
    )(x2, g, w)


def _shift_rows(x, d, fill, row):
    return jnp.where(row < d, fill, pltpu.roll(x, d, 0))


def _lru_kernel(gr_ref, f_ref, cw_ref, cb_ref, wa_ref, ba_ref, wx_ref, bx_ref, lam_ref, bf_ref,
                lru_ref, ct_ref, xprev, hprev, cprev):
    @pl.when(pl.program_id(1) == 0)
    def _():
        xprev[...] = jnp.zeros_like(xprev)
        hprev[...] = jnp.zeros_like(hprev)
        cprev[...] = jnp.zeros_like(cprev)

    ts = gr_ref.shape[1]
    w = LRU_WIDTH
    gate = gr_ref[0, :, :w]
    rec = gr_ref[0, :, w:]
    row = lax.broadcasted_iota(jnp.int32, (ts, w), 0)
    row8 = lax.broadcasted_iota(jnp.int32, (SUBLANES, w), 0)

    prev = xprev[...]
    conv = cb_ref[...] + cw_ref[CONV_WIDTH - 1:CONV_WIDTH, :] * rec
    for k in range(1, CONV_WIDTH):
        rolled = pltpu.roll(rec, k, 0)
        head = jnp.where(row8 < k, pltpu.roll(prev, k, 0), rolled[:SUBLANES])
        shifted = jnp.concatenate([head, rolled[SUBLANES:]], axis=0)
        conv = conv + cw_ref[CONV_WIDTH - 1 - k:CONV_WIDTH - k, :] * shifted
    xprev[...] = rec[ts - SUBLANES:, :]

    cb16 = conv.astype(BF16)
    r = jax.nn.sigmoid(jnp.dot(cb16, wa_ref[...], preferred_element_type=F32) + ba_ref[...])
    i = jax.nn.sigmoid(jnp.dot(cb16, wx_ref[...], preferred_element_type=F32) + bx_ref[...])
    log_a = (-LRU_C) * r * jax.nn.softplus(-lam_ref[...])
    a = jnp.exp(log_a)
    th = jnp.tanh(log_a)
    u = jnp.sqrt(-2.0 * th / (1.0 - th)) * (i * conv)

    d = 1
    while d < ts:
        u = a * _shift_rows(u, d, 0.0, row) + u
        a = a * _shift_rows(a, d, 1.0, row)
        d *= 2
    h = u + a * hprev[...]
    hprev[...] = h[ts - 1:, :]
    lru_ref[0] = (h * jax.nn.gelu(gate, approximate=True)).astype(BF16)

    c = jax.nn.log_sigmoid(f_ref[0] + bf_ref[...])
    rowc = lax.broadcasted_iota(jnp.int32, (ts, LANES), 0)
    d = 1
    while d < ts:
        c = c + _shift_rows(c, d, 0.0, rowc)
        d *= 2
    c = c + cprev[...]
    cprev[...] = c[ts - 1:, :]
    ct_ref[0] = jnp.transpose(c)[:ATT_HEADS, :]


def _lru(gr, f, cw, cb, wa, ba, wx, bx, lam, bfg, ts):
    b, s, _ = gr.shape
    w = LRU_WIDTH
    full = lambda shape: pl.BlockSpec(shape, lambda bi, ti: (0,) * len(shape))
    return pl.pallas_call(
        _lru_kernel,
        out_shape=(jax.ShapeDtypeStruct((b, s, w), BF16),
                   jax.ShapeDtypeStruct((b, ATT_HEADS, s), F32)),
        grid=(b, s // ts),
        in_specs=[pl.BlockSpec((1, ts, 2 * w), lambda bi, ti: (bi, ti, 0)),
                  pl.BlockSpec((1, ts, LANES), lambda bi, ti: (bi, ti, 0)),
                  full((CONV_WIDTH, w)), full((1, w)),
                  full((w, w)), full((1, w)), full((w, w)), full((1, w)),
                  full((1, w)), full((1, LANES))],
        out_specs=(pl.BlockSpec((1, ts, w), lambda bi, ti: (bi, ti, 0)),
                   pl.BlockSpec((1, ATT_HEADS, ts), lambda bi, ti: (bi, 0, ti))),
        scratch_shapes=[pltpu.VMEM((SUBLANES, w), F32),
                        pltpu.VMEM((1, w), F32),
                        pltpu.VMEM((1, LANES), F32)],
        compiler_params=pltpu.CompilerParams(
            dimension_semantics=("arbitrary", "arbitrary"), vmem_limit_bytes=VMEM_LIMIT),
        name="lru",
    )(gr, f, cw, cb, wa, ba, wx, bx, lam, bfg)


def _attn_kernel(q_ref, k_ref, v_ref, ct_ref, o_ref, *, t):
    i = pl.program_id(2)
    lane = lax.broadcasted_iota(jnp.int32, (1, LANES), 1)
    lo = lane < HEAD_DIM
    q = q_ref[0] * jnp.asarray(HEAD_DIM ** -0.5, BF16)
    zero = jnp.zeros_like(q)
    q_e = jnp.where(lo, q, zero)
    q_o = jnp.where(lo, zero, q)
    nt = (((1,), (1,)), ((), ()))

    def block(j, carry, masked):
        m_e, l_e, m_o, l_o, acc = carry
        start = pl.multiple_of(j * t, t)
        ks = k_ref[0, pl.ds(start, t), :]
        vs = v_ref[0, pl.ds(start, t), :]
        zv = jnp.zeros_like(vs)
        c_e = ct_ref[0, 0, 0:1, pl.ds(start, t)]
        c_o = ct_ref[0, 0, 1:2, pl.ds(start, t)]
        if masked:
            rr = lax.broadcasted_iota(jnp.int32, (t, t), 0)
            cc = lax.broadcasted_iota(jnp.int32, (t, t), 1)
            keep = cc <= rr

        def head(qh, ch, m, l):
            s = lax.dot_general(qh, ks, nt, preferred_element_type=F32) - ch
            if masked:
                s = jnp.where(keep, s, MASK_VALUE)
            m_new = jnp.maximum(m, jnp.max(s, axis=-1, keepdims=True))
            p = jnp.exp(s - m_new)
            alpha = jnp.exp(m - m_new)
            l_new = alpha * l + jnp.sum(p, axis=-1, keepdims=True)
            return p.astype(BF16), alpha, m_new, l_new

        p_e, a_e, m_e, l_e = head(q_e, c_e, m_e, l_e)
        p_o, a_o, m_o, l_o = head(q_o, c_o, m_o, l_o)
        pv = (jnp.dot(p_e, jnp.where(lo, vs, zv), preferred_element_type=F32)
              + jnp.dot(p_o, jnp.where(lo, zv, vs), preferred_element_type=F32))
        acc = acc * jnp.where(lo, a_e, a_o) + pv
        return m_e, l_e, m_o, l_o, acc

    col = lambda v: jnp.full((t, 1), v, F32)
    carry = (col(MASK_VALUE), col(0.0), col(MASK_VALUE), col(0.0), jnp.zeros((t, LANES), F32))
    carry = lax.fori_loop(0, i, lambda j, c: block(j, c, False), carry)
    _, l_e, _, l_o, acc = block(i, carry, True)
    o_ref[0] = (acc / jnp.where(lo, l_e, l_o)).astype(BF16)


def _attn(qkv, ct, t):
    b, s, _ = qkv.shape
    hp = ATT_WIDTH // LANES
    return pl.pallas_call(
        functools.partial(_attn_kernel, t=t),
        out_shape=jax.ShapeDtypeStruct((b, s, ATT_WIDTH), BF16),
        grid=(b, hp, s // t),
        in_specs=[pl.BlockSpec((1, t, LANES), lambda bi, h, i: (bi, i, h)),
                  pl.BlockSpec((1, s, LANES), lambda bi, h, i: (bi, 0, hp + h)),
                  pl.BlockSpec((1, s, LANES), lambda bi, h, i: (bi, 0, 2 * hp + h)),
                  pl.BlockSpec((1, 1, 2, s), lambda bi, h, i: (bi, h, 0, 0))],
        out_specs=pl.BlockSpec((1, t, LANES), lambda bi, h, i: (bi, i, h)),
        compiler_params=pltpu.CompilerParams(
            dimension_semantics=("arbitrary", "arbitrary", "arbitrary"),
            vmem_limit_bytes=VMEM_LIMIT),
        name="attn",
    )(qkv, qkv, qkv, ct)


def _out_route_kernel(x_ref, att_ref, lru_ref, wo_ref, g_ref, wr_ref, br_ref,
                      x1_ref, hp_ref, route_ref, cnt_ref, carry):
    @pl.when(pl.program_id(0) == 0)
    def _():
        carry[...] = jnp.zeros_like(carry)

    tm = x_ref.shape[0]
    aw = att_ref.shape[1]
    x1 = (x_ref[...]
          + jnp.dot(att_ref[...], wo_ref[:aw, :], preferred_element_type=F32)
          + jnp.dot(lru_ref[...], wo_ref[aw:, :], preferred_element_type=F32))
    x1_ref[...] = x1
    h2 = _rms(x1, g_ref[...])
    half = h2.shape[1] // 2
    hp_ref[...] = _pack_pair(h2[:, :half], h2[:, half:])

    logits = jnp.dot(h2, wr_ref[...], preferred_element_type=F32,
                     precision=lax.Precision.HIGHEST) + br_ref[...]
    lane = lax.broadcasted_iota(jnp.int32, (tm, LANES), 1)
    gl = jnp.where(lane < N_GROUPS, logits, MASK_VALUE)
    gmax = jnp.max(gl, axis=-1, keepdims=True)
    gsum = jnp.sum(jnp.exp(gl - gmax), axis=-1, keepdims=True)
    g_w = 1.0 / gsum
    g_idx = jnp.min(jnp.where(gl == gmax, lane, LANES), axis=-1, keepdims=True)
    in_group = (lane >= N_GROUPS) & (lane < N_GROUPS + N_EXPERTS) & (
        lax.shift_right_logical(lane - N_GROUPS, 3) == g_idx)
    il = jnp.where(in_group, logits, MASK_VALUE)
    m1 = jnp.max(il, axis=-1, keepdims=True)
    e1 = jnp.min(jnp.where(il == m1, lane, LANES), axis=-1, keepdims=True)
    il2 = jnp.where(lane == e1, MASK_VALUE, il)
    m2 = jnp.max(il2, axis=-1, keepdims=True)
    e2 = jnp.min(jnp.where(il2 == m2, lane, LANES), axis=-1, keepdims=True)
    ratio = jnp.exp(m2 - m1)
    w1 = g_w / (1.0 + ratio)
    w2 = w1 * ratio
    e1 = e1 - N_GROUPS
    e2 = e2 - N_GROUPS

    hot1 = lane == e1
    hot2 = lane == e2
    hot = (hot1 | hot2)
    onehot = jnp.where(hot, 1.0, 0.0).astype(BF16)
    rr = lax.broadcasted_iota(jnp.int32, (tm, tm), 0)
    cc = lax.broadcasted_iota(jnp.int32, (tm, tm), 1)
    tri = jnp.where(cc < rr, 1.0, 0.0).astype(BF16)
    before = jnp.dot(tri, onehot, preferred_element_type=F32) + carry[...]
    rank1 = jnp.sum(jnp.where(hot1, before, 0.0), axis=-1, keepdims=True)
    rank2 = jnp.sum(jnp.where(hot2, before, 0.0), axis=-1, keepdims=True)
    total = carry[...] + jnp.sum(jnp.where(hot, 1.0, 0.0), axis=0, keepdims=True)
    carry[...] = total
    cnt_ref[...] = jnp.broadcast_to(total, cnt_ref.shape)

    route = jnp.where(lane == 0, e1.astype(F32), 0.0)
    route = jnp.where(lane == 1, e2.astype(F32), route)
    route = jnp.where(lane == 2, rank1, route)
    route = jnp.where(lane == 3, rank2, route)
    route = jnp.where(lane == 4, w1, route)
    route = jnp.where(lane == 5, w2, route)
    route_ref[...] = route


def _out_route(x2, att, lru, wo, g, wr, br, tm):
    n, d = x2.shape
    aw = att.shape[1]
    row = lambda c: pl.BlockSpec((tm, c), lambda i: (i, 0))
    full = lambda shape: pl.BlockSpec(shape, lambda i: (0,) * len(shape))
    return pl.pallas_call(
        _out_route_kernel,
        out_shape=(jax.ShapeDtypeStruct((n, d), F32),
                   jax.ShapeDtypeStruct((n, d // 2), jnp.int32),
                   jax.ShapeDtypeStruct((n, LANES), F32),
                   jax.ShapeDtypeStruct((SUBLANES, LANES), F32)),
        grid=(n // tm,),
        in_specs=[row(d), row(aw), row(d - aw), full((d, d)), full((1, d)),
                  full((d, LANES)), full((1, LANES))],
        out_specs=(row(d), row(d // 2), row(LANES), full((SUBLANES, LANES))),
        scratch_shapes=[pltpu.VMEM((1, LANES), F32)],
        compiler_params=pltpu.CompilerParams(
            dimension_semantics=("arbitrary",), vmem_limit_bytes=VMEM_LIMIT),
        name="out_route",
    )(x2, att, lru, wo, g, wr, br)


def _dispatch_kernel(dest_ref, src_hbm, init_hbm, dst_hbm, sem):
    del init_hbm
    tm = dest_ref.shape[2]
    base = pl.program_id(0) * tm

    def copy(r, slot):
        return pltpu.make_async_copy(src_hbm.at[pl.ds(base + r, 1), :],
                                     dst_hbm.at[pl.ds(dest_ref[0, slot, r], 1), :], sem)

    def start(r, c):
        copy(r, 0).start()
        copy(r, 1).start()
        return c

    def wait(r, c):
        copy(r, 0).wait()
        copy(r, 1).wait()
        return c

    lax.fori_loop(0, tm, start, 0)
    lax.fori_loop(0, tm, wait, 0)


def _dispatch(dest, src, rows, tm):
    n, c = src.shape
    init = jnp.zeros((rows, c), src.dtype)
    return pl.pallas_call(
        _dispatch_kernel,
        out_shape=jax.ShapeDtypeStruct((rows, c), src.dtype),
        grid=(n // tm,),
        in_specs=[pl.BlockSpec((1, 2, tm), lambda i: (i, 0, 0), memory_space=pltpu.SMEM),
                  pl.BlockSpec(memory_space=pl.ANY),
                  pl.BlockSpec(memory_space=pl.ANY)],
        out_specs=pl.BlockSpec(memory_space=pl.ANY),
        scratch_shapes=[pltpu.SemaphoreType.DMA],
        input_output_aliases={2: 0},
        compiler_params=pltpu.CompilerParams(dimension_semantics=("arbitrary",)),
        name="dispatch",
    )(dest, src, init)


def _experts_kernel(te_ref, nu_ref, xs_ref, wg_ref, wu_ref, wd_ref, ys_ref):
    del te_ref
    i = pl.program_id(0)

    @pl.when(i < nu_ref[0])
    def _():
        lo, hi = _unpack_pair(xs_ref[...])
        half = lo.shape[1]
        lo = lo.astype(BF16)
        hi = hi.astype(BF16)
        g = (jnp.dot(lo, wg_ref[0, :half, :], preferred_element_type=F32)
             + jnp.dot(hi, wg_ref[0, half:, :], preferred_element_type=F32))
        u = (jnp.dot(lo, wu_ref[0, :half, :], preferred_element_type=F32)
             + jnp.dot(hi, wu_ref[0, half:, :], preferred_element_type=F32))
        h = (jax.nn.silu(g) * u).astype(BF16)
        y = jnp.dot(h, wd_ref[0], preferred_element_type=F32)
        yh = y.shape[1] // 2
        ys_ref[...] = _pack_pair(y[:, :yh], y[:, yh:])

    @pl.when(i >= nu_ref[0])
    def _():
        ys_ref[...] = jnp.zeros_like(ys_ref)


def _experts(tile_expert, n_used, xs, wg, wu, wd, tm):
    rows, hc = xs.shape
    d, f = wg.shape[1], wg.shape[2]
    grid_spec = pltpu.PrefetchScalarGridSpec(
        num_scalar_prefetch=2,
        grid=(rows // tm,),
        in_specs=[pl.BlockSpec((tm, hc), lambda i, te, nu: (i, 0)),
                  pl.BlockSpec((1, d, f), lambda i, te, nu: (te[i], 0, 0)),
                  pl.BlockSpec((1, d, f), lambda i, te, nu: (te[i], 0, 0)),
                  pl.BlockSpec((1, f, d), lambda i, te, nu: (te[i], 0, 0))],
        out_specs=pl.BlockSpec((tm, hc), lambda i, te, nu: (i, 0)),
    )
    return pl.pallas_call(
        _experts_kernel,
        out_shape=jax.ShapeDtypeStruct((rows, hc), jnp.int32),
        grid_spec=grid_spec,
        compiler_params=pltpu.CompilerParams(
            dimension_semantics=("arbitrary",), vmem_limit_bytes=VMEM_LIMIT),
        name="experts",
    )(tile_expert, n_used, xs, wg, wu, wd)


def _combine_kernel(dest_ref, x1_ref, route_ref, g_ref, ys_hbm, o_ref, buf, sem):
    tm = x1_ref.shape[0]

    def copy(r, slot):
        return pltpu.make_async_copy(ys_hbm.at[pl.ds(dest_ref[0, slot, r], 1), :],
                                     buf.at[slot, pl.ds(r, 1), :], sem)

    def start(r, c):
        copy(r, 0).start()
        copy(r, 1).start()
        return c

    def wait(r, c):
        copy(r, 0).wait()
        copy(r, 1).wait()
        return c

    lax.fori_loop(0, tm, start, 0)
    lax.fori_loop(0, tm, wait, 0)

    route = route_ref[...]
    w1 = route[:, 4:5]
    w2 = route[:, 5:6]
    lo1, hi1 = _unpack_pair(buf[0])
    lo2, hi2 = _unpack_pair(buf[1])
    y = jnp.concatenate([w1 * lo1 + w2 * lo2, w1 * hi1 + w2 * hi2], axis=1)
    o_ref[...] = _rms(x1_ref[...] + y, g_ref[...])


def _combine(dest, x1, route, g, ys, tm):
    n, d = x1.shape
    hc = ys.shape[1]
    return pl.pallas_call(
        _combine_kernel,
        out_shape=jax.ShapeDtypeStruct((n, d), F32),
        grid=(n // tm,),
        in_specs=[pl.BlockSpec((1, 2, tm), lambda i: (i, 0, 0), memory_space=pltpu.SMEM),
                  pl.BlockSpec((tm, d), lambda i: (i, 0)),
                  pl.BlockSpec((tm, LANES), lambda i: (i, 0)),
                  pl.BlockSpec((1, d), lambda i: (0, 0)),
                  pl.BlockSpec(memory_space=pl.ANY)],
        out_specs=pl.BlockSpec((tm, d), lambda i: (i, 0)),
        scratch_shapes=[pltpu.VMEM((2, tm, hc), jnp.int32), pltpu.SemaphoreType.DMA],
        compiler_params=pltpu.CompilerParams(
            dimension_semantics=("arbitrary",), vmem_limit_bytes=VMEM_LIMIT),
        name="combine",
    )(dest, x1, route, g, ys)


def _block_diag(w):
    nb, bd, _ = w.shape
    eye = jnp.eye(nb, dtype=w.dtype)
    return (eye[:, None, :, None] * w[:, :, None, :]).reshape(nb * bd, nb * bd)


def _layer(x2, b, s, p, tiles):
    n, d = x2.shape
    tm, ts, ta, te, tc = tiles
    sizes = (ATT_WIDTH, ATT_WIDTH, ATT_WIDTH, ATT_HEADS, LRU_WIDTH, LRU_WIDTH)
    o = [0]
    for v in sizes:
        o.append(o[-1] + v)
    w_in = p['w_in']
    w_f = jnp.pad(w_in[:, o[3]:o[4]], ((0, 0), (0, LANES - ATT_HEADS)))
    w_cat = jnp.concatenate([w_in[:, :o[3]], w_in[:, o[4]:], w_f], axis=1).astype(BF16)

    qkv, gr, f = _in_proj(x2, p['mix_norm'].reshape(1, d), w_cat, tm)

    bfg = jnp.pad(p['b_forget'], (0, LANES - ATT_HEADS)).reshape(1, LANES)
    lru, ct = _lru(gr.reshape(b, s, -1), f.reshape(b, s, LANES),
                   p['conv_w'], p['conv_b'].reshape(1, -1),
                   _block_diag(p['w_a']).astype(BF16), p['b_a'].reshape(1, -1),
                   _block_diag(p['w_x']).astype(BF16), p['b_x'].reshape(1, -1),
                   p['lru_lambda'].reshape(1, -1), bfg, ts)

    att = _attn(qkv.reshape(b, s, -1), ct.reshape(b, ATT_HEADS // 2, 2, s), ta)

    wr = jnp.concatenate(
        [p['w_group'], jnp.transpose(p['w_inner'], (1, 0, 2)).reshape(d, N_EXPERTS)], axis=1)
    wr = jnp.pad(wr, ((0, 0), (0, LANES - wr.shape[1])))
    br = jnp.concatenate([p['b_group'], p['b_inner'].reshape(-1)])
    br = jnp.pad(br, (0, LANES - br.shape[0])).reshape(1, LANES)
    x1, hpk, route, cnt = _out_route(x2, att.reshape(n, -1), lru.reshape(n, -1),
                                     p['w_out'].astype(BF16), p['ffn_norm'].reshape(1, d),
                                     wr, br, tm)

    counts = cnt[0, :N_EXPERTS].astype(jnp.int32)
    padded = ((counts + te - 1) // te) * te
    ends = jnp.cumsum(padded)
    offs = ends - padded
    e1 = route[:, 0].astype(jnp.int32)
    e2 = route[:, 1].astype(jnp.int32)
    d1 = offs[e1] + route[:, 2].astype(jnp.int32)
    d2 = offs[e2] + route[:, 3].astype(jnp.int32)
    rows = 2 * n + N_EXPERTS * te
    n_tiles = rows // te
    tile_start = jnp.arange(n_tiles, dtype=jnp.int32) * te
    tile_expert = jnp.minimum(
        jnp.sum((tile_start[:, None] >= ends[None, :]).astype(jnp.int32), axis=1), N_EXPERTS - 1)
    n_used = (ends[-1:] // te).astype(jnp.int32)

    def dest_blocks(t):
        return jnp.stack([d1.reshape(n // t, t), d2.reshape(n // t, t)], axis=1)

    xs = _dispatch(dest_blocks(tm), hpk, rows, tm)
    f_e = p['w_gate'].shape[-1]
    ys = _experts(tile_expert, n_used, xs,
                  p['w_gate'].reshape(N_EXPERTS, d, f_e).astype(BF16),
                  p['w_up'].reshape(N_EXPERTS, d, f_e).astype(BF16),
                  p['w_down'].reshape(N_EXPERTS, f_e, d).astype(BF16), te)
    return dest_blocks(tc), x1, route, ys


def kernel(x, mix_norm, w_in, b_forget, conv_w, conv_b, w_a, b_a, w_x, b_x, lru_lambda, w_out,
           ffn_norm, w_group, b_group, w_inner, b_inner, w_gate, w_up, w_down, final_norm):
    b, s, d = x.shape
    depth = w_in.shape[0]
    assert depth == 1, "the final rmsnorm is fused into the last layer's combine stage"
    tiles = (min(512, s), min(512, s), min(512, s), 256, 256)
    names = ('mix_norm', 'w_in', 'b_forget', 'conv_w', 'conv_b', 'w_a', 'b_a', 'w_x', 'b_x',
             'lru_lambda', 'w_out', 'ffn_norm', 'w_group', 'b_group', 'w_inner', 'b_inner',
             'w_gate', 'w_up', 'w_down')
    vals = (mix_norm, w_in, b_forget, conv_w, conv_b, w_a, b_a, w_x, b_x, lru_lambda, w_out,
            ffn_norm, w_group, b_group, w_inner, b_inner, w_gate, w_up, w_down)
    p = {k: v[0] for k, v in zip(names, vals)}
    dest, x1, route, ys = _layer(x.reshape(b * s, d), b, s, p, tiles)
    out = _combine(dest, x1, route, final_norm.reshape(1, d), ys, tiles[4])
    return out.reshape(b, s, d)
```

```python
import functools

import jax
import jax.numpy as jnp
from jax import lax
from jax.experimental import pallas as pl
from jax.experimental.pallas import tpu as pltpu

ATT_HEADS = 8
HEAD_DIM = 64
ATT_WIDTH = ATT_HEADS * HEAD_DIM
LRU_WIDTH = 512
LRU_BLOCKS = 8
CONV_WIDTH = 4
LRU_C = 8.0
N_GROUPS = 4
EXPERTS_PER_GROUP = 8
N_EXPERTS = N_GROUPS * EXPERTS_PER_GROUP
D_EXPERT = 256
RMS_EPS = 1e-6
LANES = 128
SUBLANES = 8
CHUNK = 16
TILE_CHUNKS = 16
MASK_VALUE = -1e30
VMEM_LIMIT = 48 * 1024 * 1024

BF16 = jnp.bfloat16
F32 = jnp.float32


def _rms(x, g):
    inv = lax.rsqrt(jnp.mean(x * x, axis=-1, keepdims=True) + RMS_EPS)
    return x * inv * g


def _block_rows(tm):
    return 2 * tm + N_EXPERTS * CHUNK


def _in_proj_kernel(x_ref, g_ref, w_ref, qkv_ref, gr_ref, f_ref):
    h = _rms(x_ref[...], g_ref[...]).astype(BF16)
    nq = qkv_ref.shape[1]
    ng = gr_ref.shape[1]
    qkv_ref[...] = jnp.dot(h, w_ref[:, :nq], preferred_element_type=F32).astype(BF16)
    gr_ref[...] = jnp.dot(h, w_ref[:, nq:nq + ng], preferred_element_type=F32)
    f_ref[...] = jnp.dot(h, w_ref[:, nq + ng:], preferred_element_type=F32)


def _in_proj(x2, g, w, tm):
    n, d = x2.shape
    nq, ng = 3 * ATT_WIDTH, 2 * LRU_WIDTH
    return pl.pallas_call(
        _in_proj_kernel,
        out_shape=(jax.ShapeDtypeStruct((n, nq), BF16),
                   jax.ShapeDtypeStruct((n, ng), F32),
                   jax.ShapeDtypeStruct((n, LANES), F32)),
        grid=(n // tm,),
        in_specs=[pl.BlockSpec((tm, d), lambda i: (i, 0)),
                  pl.BlockSpec((1, d), lambda i: (0, 0)),
                  pl.BlockSpec(w.shape, lambda i: (0, 0))],
        out_specs=(pl.BlockSpec((tm, nq), lambda i: (i, 0)),
                   pl.BlockSpec((tm, ng), lambda i: (i, 0)),
                   pl.BlockSpec((tm, LANES), lambda i: (i, 0))),
        compiler_params=pltpu.CompilerParams(
            dimension_semantics=("arbitrary",), vmem_limit_bytes=VMEM_LIMIT),
        name="in_proj",
    )(x2, g, w)


def _shift_rows(x, d, fill, row):
    return jnp.where(row < d, fill, pltpu.roll(x, d, 0))


def _lru_kernel(gr_ref, f_ref, cw_ref, cb_ref, wa_ref, ba_ref, wx_ref, bx_ref, lam_ref, bf_ref,
                lru_ref, ct_ref, xprev, hprev, cprev):
    @pl.when(pl.program_id(1) == 0)
    def _():
        xprev[...] = jnp.zeros_like(xprev)
        hprev[...] = jnp.zeros_like(hprev)
        cprev[...] = jnp.zeros_like(cprev)

    ts = gr_ref.shape[1]
    w = LRU_WIDTH
    gate = gr_ref[0, :, :w]
    rec = gr_ref[0, :, w:]
    row = lax.broadcasted_iota(jnp.int32, (ts, w), 0)
    row8 = lax.broadcasted_iota(jnp.int32, (SUBLANES, w), 0)

    prev = xprev[...]
    conv = cb_ref[...] + cw_ref[CONV_WIDTH - 1:CONV_WIDTH, :] * rec
    for k in range(1, CONV_WIDTH):
        rolled = pltpu.roll(rec, k, 0)
        head = jnp.where(row8 < k, pltpu.roll(prev, k, 0), rolled[:SUBLANES])
        shifted = jnp.concatenate([head, rolled[SUBLANES:]], axis=0)
        conv = conv + cw_ref[CONV_WIDTH - 1 - k:CONV_WIDTH - k, :] * shifted
    xprev[...] = rec[ts - SUBLANES:, :]

    cb16 = conv.astype(BF16)
    r = jax.nn.sigmoid(jnp.dot(cb16, wa_ref[...], preferred_element_type=F32) + ba_ref[...])
    i = jax.nn.sigmoid(jnp.dot(cb16, wx_ref[...], preferred_element_type=F32) + bx_ref[...])
    log_a = (-LRU_C) * r * jax.nn.softplus(-lam_ref[...])
    a = jnp.exp(log_a)
    th = jnp.tanh(log_a)
    u = jnp.sqrt(-2.0 * th / (1.0 - th)) * (i * conv)

    d = 1
    while d < ts:
        u = a * _shift_rows(u, d, 0.0, row) + u
        a = a * _shift_rows(a, d, 1.0, row)
        d *= 2
    h = u + a * hprev[...]
    hprev[...] = h[ts - 1:, :]
    lru_ref[0] = (h * jax.nn.gelu(gate, approximate=True)).astype(BF16)

    c = jax.nn.log_sigmoid(f_ref[0] + bf_ref[...])
    rowc = lax.broadcasted_iota(jnp.int32, (ts, LANES), 0)
    d = 1
    while d < ts:
        c = c + _shift_rows(c, d, 0.0, rowc)
        d *= 2
    c = c + cprev[...]
    cprev[...] = c[ts - 1:, :]
    ct_ref[0] = jnp.transpose(c)[:ATT_HEADS, :]


def _lru(gr, f, cw, cb, wa, ba, wx, bx, lam, bfg, ts):
    b, s, _ = gr.shape
    w = LRU_WIDTH
    full = lambda shape: pl.BlockSpec(shape, lambda bi, ti: (0,) * len(shape))
    return pl.pallas_call(
        _lru_kernel,
        out_shape=(jax.ShapeDtypeStruct((b, s, w), BF16),
                   jax.ShapeDtypeStruct((b, ATT_HEADS, s), F32)),
        grid=(b, s // ts),
        in_specs=[pl.BlockSpec((1, ts, 2 * w), lambda bi, ti: (bi, ti, 0)),
                  pl.BlockSpec((1, ts, LANES), lambda bi, ti: (bi, ti, 0)),
                  full((CONV_WIDTH, w)), full((1, w)),
                  full((w, w)), full((1, w)), full((w, w)), full((1, w)),
                  full((1, w)), full((1, LANES))],
        out_specs=(pl.BlockSpec((1, ts, w), lambda bi, ti: (bi, ti, 0)),
                   pl.BlockSpec((1, ATT_HEADS, ts), lambda bi, ti: (bi, 0, ti))),
        scratch_shapes=[pltpu.VMEM((SUBLANES, w), F32),
                        pltpu.VMEM((1, w), F32),
                        pltpu.VMEM((1, LANES), F32)],
        compiler_params=pltpu.CompilerParams(
            dimension_semantics=("arbitrary", "arbitrary"), vmem_limit_bytes=VMEM_LIMIT),
        name="lru",
    )(gr, f, cw, cb, wa, ba, wx, bx, lam, bfg)


def _attn_kernel(q_ref, k_ref, v_ref, ct_ref, o_ref, *, t):
    i = pl.program_id(2)
    lane = lax.broadcasted_iota(jnp.int32, (1, LANES), 1)
    lo = lane < HEAD_DIM
    q = q_ref[0] * jnp.asarray(HEAD_DIM ** -0.5, BF16)
    zero = jnp.zeros_like(q)
    q_e = jnp.where(lo, q, zero)
    q_o = jnp.where(lo, zero, q)
    nt = (((1,), (1,)), ((), ()))

    def block(j, carry, masked):
        m_e, l_e, m_o, l_o, acc = carry
        start = pl.multiple_of(j * t, t)
        ks = k_ref[0, pl.ds(start, t), :]
        vs = v_ref[0, pl.ds(start, t), :]
        zv = jnp.zeros_like(vs)
        c_e = ct_ref[0, 0, 0:1, pl.ds(start, t)]
        c_o = ct_ref[0, 0, 1:2, pl.ds(start, t)]
        if masked:
            rr = lax.broadcasted_iota(jnp.int32, (t, t), 0)
            cc = lax.broadcasted_iota(jnp.int32, (t, t), 1)
            keep = cc <= rr

        def head(qh, ch, m, l):
            s = lax.dot_general(qh, ks, nt, preferred_element_type=F32) - ch
            if masked:
                s = jnp.where(keep, s, MASK_VALUE)
            m_new = jnp.maximum(m, jnp.max(s, axis=-1, keepdims=True))
            p = jnp.exp(s - m_new)
            alpha = jnp.exp(m - m_new)
            l_new = alpha * l + jnp.sum(p, axis=-1, keepdims=True)
            return p.astype(BF16), alpha, m_new, l_new

        p_e, a_e, m_e, l_e = head(q_e, c_e, m_e, l_e)
        p_o, a_o, m_o, l_o = head(q_o, c_o, m_o, l_o)
        pv = (jnp.dot(p_e, jnp.where(lo, vs, zv), preferred_element_type=F32)
              + jnp.dot(p_o, jnp.where(lo, zv, vs), preferred_element_type=F32))
        acc = acc * jnp.where(lo, a_e, a_o) + pv
        return m_e, l_e, m_o, l_o, acc

    col = lambda v: jnp.full((t, 1), v, F32)
    carry = (col(MASK_VALUE), col(0.0), col(MASK_VALUE), col(0.0), jnp.zeros((t, LANES), F32))
    carry = lax.fori_loop(0, i, lambda j, c: block(j, c, False), carry)
    _, l_e, _, l_o, acc = block(i, carry, True)
    o_ref[0] = (acc / jnp.where(lo, l_e, l_o)).astype(BF16)


def _attn(qkv, ct, t):
    b, s, _ = qkv.shape
    hp = ATT_WIDTH // LANES
    return pl.pallas_call(
        functools.partial(_attn_kernel, t=t),
        out_shape=jax.ShapeDtypeStruct((b, s, ATT_WIDTH), BF16),
        grid=(b, hp, s // t),
        in_specs=[pl.BlockSpec((1, t, LANES), lambda bi, h, i: (bi, i, h)),
                  pl.BlockSpec((1, s, LANES), lambda bi, h, i: (bi, 0, hp + h)),
                  pl.BlockSpec((1, s, LANES), lambda bi, h, i: (bi, 0, 2 * hp + h)),
                  pl.BlockSpec((1, 1, 2, s), lambda bi, h, i: (bi, h, 0, 0))],
        out_specs=pl.BlockSpec((1, t, LANES), lambda bi, h, i: (bi, i, h)),
        compiler_params=pltpu.CompilerParams(
            dimension_semantics=("arbitrary", "arbitrary", "arbitrary"),
            vmem_limit_bytes=VMEM_LIMIT),
        name="attn",
    )(qkv, qkv, qkv, ct)


def _out_route_kernel(x_ref, att_ref, lru_ref, wo_ref, g_ref, wr_ref, br_ref,
                      x1_ref, xs_ref, tok_ref, nch_ref):
    tm = x_ref.shape[0]
    aw = att_ref.shape[1]
    x1 = (x_ref[...]
          + jnp.dot(att_ref[...], wo_ref[:aw, :], preferred_element_type=F32)
          + jnp.dot(lru_ref[...], wo_ref[aw:, :], preferred_element_type=F32))
    x1_ref[...] = x1
    h2 = _rms(x1, g_ref[...])

    logits = jnp.dot(h2, wr_ref[...], preferred_element_type=F32,
                     precision=lax.Precision.HIGHEST) + br_ref[...]
    lane = lax.broadcasted_iota(jnp.int32, (tm, LANES), 1)
    gl = jnp.where(lane < N_GROUPS, logits, MASK_VALUE)
    gmax = jnp.max(gl, axis=-1, keepdims=True)
    gsum = jnp.sum(jnp.exp(gl - gmax), axis=-1, keepdims=True)
    g_w = 1.0 / gsum
    g_idx = jnp.min(jnp.where(gl == gmax, lane, LANES), axis=-1, keepdims=True)
    in_group = (lane >= N_GROUPS) & (lane < N_GROUPS + N_EXPERTS) & (
        lax.shift_right_logical(lane - N_GROUPS, 3) == g_idx)
    il = jnp.where(in_group, logits, MASK_VALUE)
    m1 = jnp.max(il, axis=-1, keepdims=True)
    e1 = jnp.min(jnp.where(il == m1, lane, LANES), axis=-1, keepdims=True)
    il2 = jnp.where(lane == e1, MASK_VALUE, il)
    m2 = jnp.max(il2, axis=-1, keepdims=True)
    e2 = jnp.min(jnp.where(il2 == m2, lane, LANES), axis=-1, keepdims=True)
    ratio = jnp.exp(m2 - m1)
    w1 = g_w / (1.0 + ratio)
    w2 = w1 * ratio
    hot1 = lane == e1 - N_GROUPS
    hot2 = lane == e2 - N_GROUPS

    onehot = jnp.where(hot1 | hot2, 1.0, 0.0)
    rr = lax.broadcasted_iota(jnp.int32, (tm, tm), 0)
    cc = lax.broadcasted_iota(jnp.int32, (tm, tm), 1)
    tri = jnp.where(cc < rr, 1.0, 0.0).astype(BF16)
    before = jnp.dot(tri, onehot.astype(BF16), preferred_element_type=F32)
    cnt = jnp.sum(onehot, axis=0, keepdims=True)
    nch = jnp.floor((cnt + (CHUNK - 1)) * (1.0 / CHUNK))
    r128 = lax.broadcasted_iota(jnp.int32, (LANES, LANES), 0)
    c128 = lax.broadcasted_iota(jnp.int32, (LANES, LANES), 1)
    upper = jnp.where(r128 < c128, 1.0, 0.0).astype(BF16)
    nch8 = jnp.broadcast_to(nch, (SUBLANES, LANES))
    seg_off = CHUNK * jnp.dot(nch8.astype(BF16), upper, preferred_element_type=F32)[0:1, :]
    pos = seg_off + before
    pos1 = jnp.sum(jnp.where(hot1, pos, 0.0), axis=-1, keepdims=True)
    pos2 = jnp.sum(jnp.where(hot2, pos, 0.0), axis=-1, keepdims=True)
    nch_ref[0] = nch8

    tok = jnp.where(lane == 0, pos1, 0.0)
    tok = jnp.where(lane == 1, pos2, tok)
    tok = jnp.where(lane == 2, w1, tok)
    tok = jnp.where(lane == 3, w2, tok)
    tok_ref[...] = tok

    tok_t = jnp.transpose(tok)
    p1 = tok_t[0:1, :].astype(jnp.int32)
    p2 = tok_t[1:2, :].astype(jnp.int32)
    rows = xs_ref.shape[0]
    ri = lax.broadcasted_iota(jnp.int32, (rows, tm), 0)
    perm = jnp.where((ri == p1) | (ri == p2), 1.0, 0.0).astype(BF16)
    xs_ref[...] = jnp.dot(perm, h2.astype(BF16), preferred_element_type=F32).astype(BF16)


def _out_route(x2, att, lru, wo, g, wr, br, tm):
    n, d = x2.shape
    aw = att.shape[1]
    nblk = n // tm
    rows = _block_rows(tm)
    row = lambda c: pl.BlockSpec((tm, c), lambda i: (i, 0))
    full = lambda shape: pl.BlockSpec(shape, lambda i: (0,) * len(shape))
    return pl.pallas_call(
        _out_route_kernel,
        out_shape=(jax.ShapeDtypeStruct((n, d), F32),
                   jax.ShapeDtypeStruct((nblk * rows, d), BF16),
                   jax.ShapeDtypeStruct((n, LANES), F32),
                   jax.ShapeDtypeStruct((nblk, SUBLANES, LANES), F32)),
        grid=(nblk,),
        in_specs=[row(d), row(aw), row(d - aw), full((d, d)), full((1, d)),
                  full((d, LANES)), full((1, LANES))],
        out_specs=(row(d), pl.BlockSpec((rows, d), lambda i: (i, 0)), row(LANES),
                   pl.BlockSpec((1, SUBLANES, LANES), lambda i: (i, 0, 0))),
        compiler_params=pltpu.CompilerParams(
            dimension_semantics=("arbitrary",), vmem_limit_bytes=VMEM_LIMIT),
        name="out_route",
    )(x2, att, lru, wo, g, wr, br)


def _experts_kernel(te_ref, nu_ref, src_ref, nv_ref, xs_hbm, wg_ref, wu_ref, wd_ref, init_hbm,
                    ys_hbm, xbuf, ybuf, gsem, ssem):
    del te_ref, init_hbm
    i = pl.program_id(0)
    nu = nu_ref[0]

    def gather(tile, k):
        row = pl.multiple_of(src_ref[tile * TILE_CHUNKS + k], CHUNK)
        slot = tile % 2
        return pltpu.make_async_copy(xs_hbm.at[pl.ds(row, CHUNK), :],
                                     xbuf.at[slot, pl.ds(k * CHUNK, CHUNK), :], gsem.at[slot])

    def scatter(tile, k):
        row = pl.multiple_of(src_ref[tile * TILE_CHUNKS + k], CHUNK)
        slot = tile % 2
        return pltpu.make_async_copy(ybuf.at[slot, pl.ds(k * CHUNK, CHUNK), :],
                                     ys_hbm.at[pl.ds(row, CHUNK), :], ssem.at[slot])

    def start_gathers(tile):
        for k in range(TILE_CHUNKS):
            gather(tile, k).start()

    def wait_scatters(tile):
        for k in range(TILE_CHUNKS):
            @pl.when(k < nv_ref[tile])
            def _():
                scatter(tile, k).wait()

    @pl.when((i == 0) & (nu > 0))
    def _():
        start_gathers(0)

    @pl.when(i + 1 < nu)
    def _():
        start_gathers(i + 1)

    @pl.when(i < nu)
    def _():
        for k in range(TILE_CHUNKS):
            gather(i, k).wait()
        slot = i % 2
        x = xbuf[slot]
        g = jnp.dot(x, wg_ref[0], preferred_element_type=F32)
        u = jnp.dot(x, wu_ref[0], preferred_element_type=F32)
        h = (jax.nn.silu(g) * u).astype(BF16)
        ybuf[slot] = jnp.dot(h, wd_ref[0], preferred_element_type=F32).astype(BF16)
        for k in range(TILE_CHUNKS):
            @pl.when(k < nv_ref[i])
            def _():
                scatter(i, k).start()

        @pl.when(i > 0)
        def _():
            wait_scatters(i - 1)

        @pl.when(i == nu - 1)
        def _():
            wait_scatters(i)


def _experts(tile_expert, n_used, chunk_src, n_valid, xs, wg, wu, wd):
    rows, d = xs.shape
    f = wg.shape[2]
    tm = TILE_CHUNKS * CHUNK
    n_tiles = tile_expert.shape[0]
    wspec = lambda shape: pl.BlockSpec(shape, lambda i, te, nu, src, nv: (te[i], 0, 0))
    grid_spec = pltpu.PrefetchScalarGridSpec(
        num_scalar_prefetch=4,
        grid=(n_tiles,),
        in_specs=[pl.BlockSpec(memory_space=pl.ANY),
                  wspec((1, d, f)), wspec((1, d, f)), wspec((1, f, d)),
                  pl.BlockSpec(memory_space=pl.ANY)],
        out_specs=pl.BlockSpec(memory_space=pl.ANY),
        scratch_shapes=[pltpu.VMEM((2, tm, d), BF16), pltpu.VMEM((2, tm, d), BF16),
                        pltpu.SemaphoreType.DMA((2,)), pltpu.SemaphoreType.DMA((2,))],
    )
    return pl.pallas_call(
        _experts_kernel,
        out_shape=jax.ShapeDtypeStruct((rows, d), BF16),
        grid_spec=grid_spec,
        input_output_aliases={8: 0},
        compiler_params=pltpu.CompilerParams(
            dimension_semantics=("arbitrary",), vmem_limit_bytes=VMEM_LIMIT),
        name="experts",
    )(tile_expert, n_used, chunk_src, n_valid, xs, wg, wu, wd, jnp.zeros((rows, d), BF16))


def _combine_kernel(x1_ref, tok_ref, g_ref, ys_ref, o_ref):
    tm = x1_ref.shape[0]
    rows = ys_ref.shape[0]
    tok = tok_ref[...]
    pos1 = tok[:, 0:1].astype(jnp.int32)
    pos2 = tok[:, 1:2].astype(jnp.int32)
    ci = lax.broadcasted_iota(jnp.int32, (tm, rows), 1)
    unperm = jnp.where(ci == pos1, tok[:, 2:3], jnp.where(ci == pos2, tok[:, 3:4], 0.0))
    y = jnp.dot(unperm.astype(BF16), ys_ref[...], preferred_element_type=F32)
    o_ref[...] = _rms(x1_ref[...] + y, g_ref[...])


def _combine(x1, tok, g, ys, tm):
    n, d = x1.shape
    rows = _block_rows(tm)
    return pl.pallas_call(
        _combine_kernel,
        out_shape=jax.ShapeDtypeStruct((n, d), F32),
        grid=(n // tm,),
        in_specs=[pl.BlockSpec((tm, d), lambda i: (i, 0)),
                  pl.BlockSpec((tm, LANES), lambda i: (i, 0)),
                  pl.BlockSpec((1, d), lambda i: (0, 0)),
                  pl.BlockSpec((rows, d), lambda i: (i, 0))],
        out_specs=pl.BlockSpec((tm, d), lambda i: (i, 0)),
        compiler_params=pltpu.CompilerParams(
            dimension_semantics=("arbitrary",), vmem_limit_bytes=VMEM_LIMIT),
        name="combine",
    )(x1, tok, g, ys)


def _block_diag(w):
    nb, bd, _ = w.shape
    eye = jnp.eye(nb, dtype=w.dtype)
    return (eye[:, None, :, None] * w[:, :, None, :]).reshape(nb * bd, nb * bd)


def _tile_tables(nch, rows):
    nblk = nch.shape[0]
    ex_cum = lambda a, axis: jnp.cumsum(a, axis=axis) - a
    ch_off = ex_cum(nch, 1)
    blk_cum = ex_cum(nch, 0)
    tot = jnp.sum(nch, axis=0)
    tiles_e = (tot + TILE_CHUNKS - 1) // TILE_CHUNKS
    t_end = jnp.cumsum(tiles_e)
    t_start = t_end - tiles_e
    max_chunks = nblk * ((rows - N_EXPERTS * CHUNK) // CHUNK + N_EXPERTS)
    n_tiles = pl.cdiv(max_chunks, TILE_CHUNKS) + N_EXPERTS
    tid = jnp.arange(n_tiles, dtype=jnp.int32)
    te = jnp.minimum(jnp.sum((tid[:, None] >= t_end[None, :]).astype(jnp.int32), axis=1),
                     N_EXPERTS - 1)
    q = (tid - t_start[te])[:, None] * TILE_CHUNKS + jnp.arange(TILE_CHUNKS, dtype=jnp.int32)[None, :]
    valid = (tid < t_end[-1])[:, None] & (q < tot[te][:, None])
    seg_start = blk_cum.T[te]
    seg_end = seg_start + nch.T[te]
    bidx = jnp.minimum(jnp.sum((q[:, :, None] >= seg_end[:, None, :]).astype(jnp.int32), axis=-1),
                       nblk - 1)
    start_b = jnp.take_along_axis(seg_start, bidx, axis=1)
    off_b = jnp.take_along_axis(ch_off.T[te], bidx, axis=1)
    src = bidx * rows + CHUNK * (off_b + q - start_b)
    src = jnp.where(valid, src, rows - CHUNK).astype(jnp.int32)
    n_valid = jnp.sum(valid.astype(jnp.int32), axis=1)
    return te, t_end[-1:].astype(jnp.int32), src.reshape(-1), n_valid


def kernel(x, mix_norm, w_in, b_forget, conv_w, conv_b, w_a, b_a, w_x, b_x, lru_lambda, w_out,
           ffn_norm, w_group, b_group, w_inner, b_inner, w_gate, w_up, w_down, final_norm):
    b, s, d = x.shape
    assert w_in.shape[0] == 1, "the final rmsnorm is fused into the single layer's combine stage"
    n = b * s
    tm = min(512, s)
    x2 = x.reshape(n, d)

    sizes = (ATT_WIDTH, ATT_WIDTH, ATT_WIDTH, ATT_HEADS, LRU_WIDTH, LRU_WIDTH)
    o = [0]
    for v in sizes:
        o.append(o[-1] + v)
    w_f = jnp.pad(w_in[0][:, o[3]:o[4]], ((0, 0), (0, LANES - ATT_HEADS)))
    w_cat = jnp.concatenate([w_in[0][:, :o[3]], w_in[0][:, o[4]:], w_f], axis=1).astype(BF16)
    qkv, gr, f = _in_proj(x2, mix_norm.reshape(1, d), w_cat, tm)

    bfg = jnp.pad(b_forget[0], (0, LANES - ATT_HEADS)).reshape(1, LANES)
    lru, ct = _lru(gr.reshape(b, s, -1), f.reshape(b, s, LANES),
                   conv_w[0], conv_b.reshape(1, -1),
                   _block_diag(w_a[0]).astype(BF16), b_a.reshape(1, -1),
                   _block_diag(w_x[0]).astype(BF16), b_x.reshape(1, -1),
                   lru_lambda.reshape(1, -1), bfg, tm)

    att = _attn(qkv.reshape(b, s, -1), ct.reshape(b, ATT_HEADS // 2, 2, s), tm)

    wr = jnp.concatenate(
        [w_group[0], jnp.transpose(w_inner[0], (1, 0, 2)).reshape(d, N_EXPERTS)], axis=1)
    wr = jnp.pad(wr, ((0, 0), (0, LANES - wr.shape[1])))
    br = jnp.concatenate([b_group[0], b_inner[0].reshape(-1)])
    br = jnp.pad(br, (0, LANES - br.shape[0])).reshape(1, LANES)
    x1, xs, tok, nch = _out_route(x2, att.reshape(n, -1), lru.reshape(n, -1),
                                  w_out[0].astype(BF16), ffn_norm.reshape(1, d), wr, br, tm)

    tables = _tile_tables(nch[:, 0, :N_EXPERTS].astype(jnp.int32), _block_rows(tm))
    fe = w_gate.shape[-1]
    ys = _experts(*tables, xs,
                  w_gate.reshape(N_EXPERTS, d, fe).astype(BF16),
                  w_up.reshape(N_EXPERTS, d, fe).astype(BF16),
                  w_down.reshape(N_EXPERTS, fe, d).astype(BF16))
    out = _combine(x1, tok, final_norm.reshape(1, d), ys, tm)
    return out.reshape(b, s, d)
```

```python
import functools

import jax
import jax.numpy as jnp
from jax import lax
from jax.experimental import pallas as pl
from jax.experimental.pallas import tpu as pltpu

ATT_HEADS = 8
HEAD_DIM = 64
ATT_WIDTH = ATT_HEADS * HEAD_DIM
LRU_WIDTH = 512
LRU_BLOCKS = 8
CONV_WIDTH = 4
LRU_C = 8.0
N_GROUPS = 4
EXPERTS_PER_GROUP = 8
N_EXPERTS = N_GROUPS * EXPERTS_PER_GROUP
D_EXPERT = 256
RMS_EPS = 1e-6
LANES = 128
SUBLANES = 8
CHUNK = 16
TILE_CHUNKS = 16
MASK_VALUE = -1e30
LOG2E = 1.4426950408889634
DECAY_PARTS = 3
VMEM_LIMIT = 48 * 1024 * 1024

BF16 = jnp.bfloat16
F32 = jnp.float32


def _rms(x, g):
    inv = lax.rsqrt(jnp.mean(x * x, axis=-1, keepdims=True) + RMS_EPS)
    return x * inv * g


def _block_rows(tm):
    return 2 * tm + N_EXPERTS * CHUNK


def _in_proj_kernel(x_ref, g_ref, w_ref, qkv_ref, gr_ref, f_ref):
    h = _rms(x_ref[...], g_ref[...]).astype(BF16)
    nq = qkv_ref.shape[1]
    ng = gr_ref.shape[1]
    aw = ATT_WIDTH
    q = jnp.dot(h, w_ref[:, :aw], preferred_element_type=F32) * (HEAD_DIM ** -0.5 * LOG2E)
    qkv_ref[:, :aw] = q.astype(BF16)
    qkv_ref[:, aw:] = jnp.dot(h, w_ref[:, aw:nq], preferred_element_type=F32).astype(BF16)
    gr_ref[...] = jnp.dot(h, w_ref[:, nq:nq + ng], preferred_element_type=F32)
    f_ref[...] = jnp.dot(h, w_ref[:, nq + ng:], preferred_element_type=F32)


def _in_proj(x2, g, w, tm):
    n, d = x2.shape
    nq, ng = 3 * ATT_WIDTH, 2 * LRU_WIDTH
    return pl.pallas_call(
        _in_proj_kernel,
        out_shape=(jax.ShapeDtypeStruct((n, nq), BF16),
                   jax.ShapeDtypeStruct((n, ng), F32),
                   jax.ShapeDtypeStruct((n, LANES), F32)),
        grid=(n // tm,),
        in_specs=[pl.BlockSpec((tm, d), lambda i: (i, 0)),
                  pl.BlockSpec((1, d), lambda i: (0, 0)),
                  pl.BlockSpec(w.shape, lambda i: (0, 0))],
        out_specs=(pl.BlockSpec((tm, nq), lambda i: (i, 0)),
                   pl.BlockSpec((tm, ng), lambda i: (i, 0)),
                   pl.BlockSpec((tm, LANES), lambda i: (i, 0))),
        compiler_params=pltpu.CompilerParams(
            dimension_semantics=("arbitrary",), vmem_limit_bytes=VMEM_LIMIT),
        name="in_proj",
    )(x2, g, w)


def _shift_rows(x, d, fill, row):
    return jnp.where(row < d, fill, pltpu.roll(x, d, 0))


def _lru_kernel(gr_ref, f_ref, cw_ref, cb_ref, wa_ref, ba_ref, wx_ref, bx_ref, lam_ref, bf_ref,
                lru_ref, cs_ref, xprev, hprev, cprev):
    @pl.when(pl.program_id(1) == 0)
    def _():
        xprev[...] = jnp.zeros_like(xprev)
        hprev[...] = jnp.zeros_like(hprev)
        cprev[...] = jnp.zeros_like(cprev)

    ts = gr_ref.shape[1]
    w = LRU_WIDTH
    gate = gr_ref[0, :, :w]
    rec = gr_ref[0, :, w:]
    row = lax.broadcasted_iota(jnp.int32, (ts, w), 0)
    row8 = lax.broadcasted_iota(jnp.int32, (SUBLANES, w), 0)

    prev = xprev[...]
    conv = cb_ref[...] + cw_ref[CONV_WIDTH - 1:CONV_WIDTH, :] * rec
    for k in range(1, CONV_WIDTH):
        rolled = pltpu.roll(rec, k, 0)
        head = jnp.where(row8 < k, pltpu.roll(prev, k, 0), rolled[:SUBLANES])
        shifted = jnp.concatenate([head, rolled[SUBLANES:]], axis=0)
        conv = conv + cw_ref[CONV_WIDTH - 1 - k:CONV_WIDTH - k, :] * shifted
    xprev[...] = rec[ts - SUBLANES:, :]

    cb16 = conv.astype(BF16)
    r = jax.nn.sigmoid(jnp.dot(cb16, wa_ref[...], preferred_element_type=F32) + ba_ref[...])
    i = jax.nn.sigmoid(jnp.dot(cb16, wx_ref[...], preferred_element_type=F32) + bx_ref[...])
    log_a = (-LRU_C) * r * jax.nn.softplus(-lam_ref[...])
    a = jnp.exp(log_a)
    th = jnp.tanh(log_a)
    u = jnp.sqrt(-2.0 * th / (1.0 - th)) * (i * conv)

    d = 1
    while d < ts:
        u = a * _shift_rows(u, d, 0.0, row) + u
        a = a * _shift_rows(a, d, 1.0, row)
        d *= 2
    h = u + a * hprev[...]
    hprev[...] = h[ts - 1:, :]
    lru_ref[0] = (h * jax.nn.gelu(gate, approximate=True)).astype(BF16)

    c = jax.nn.log_sigmoid(f_ref[0] + bf_ref[...])
    rowc = lax.broadcasted_iota(jnp.int32, (ts, LANES), 0)
    lanec = lax.broadcasted_iota(jnp.int32, (ts, LANES), 1)
    d = 1
    while d < ts:
        c = c + _shift_rows(c, d, 0.0, rowc)
        d *= 2
    c = c + cprev[...]
    cprev[...] = c[ts - 1:, :]
    rest = c * (-LOG2E)
    parts = jnp.zeros_like(rest)
    for j in range(DECAY_PARTS):
        piece = rest.astype(BF16).astype(F32)
        rest = rest - piece
        shifted = piece if j == 0 else pltpu.roll(piece, ATT_HEADS * j, 1)
        parts = jnp.where((lanec >= ATT_HEADS * j) & (lanec < ATT_HEADS * (j + 1)), shifted, parts)
    cs_ref[0] = parts.astype(BF16)


def _lru(gr, f, cw, cb, wa, ba, wx, bx, lam, bfg, ts):
    b, s, _ = gr.shape
    w = LRU_WIDTH
    full = lambda shape: pl.BlockSpec(shape, lambda bi, ti: (0,) * len(shape))
    return pl.pallas_call(
        _lru_kernel,
        out_shape=(jax.ShapeDtypeStruct((b, s, w), BF16),
                   jax.ShapeDtypeStruct((b, s, LANES), BF16)),
        grid=(b, s // ts),
        in_specs=[pl.BlockSpec((1, ts, 2 * w), lambda bi, ti: (bi, ti, 0)),
                  pl.BlockSpec((1, ts, LANES), lambda bi, ti: (bi, ti, 0)),
                  full((CONV_WIDTH, w)), full((1, w)),
                  full((w, w)), full((1, w)), full((w, w)), full((1, w)),
                  full((1, w)), full((1, LANES))],
        out_specs=(pl.BlockSpec((1, ts, w), lambda bi, ti: (bi, ti, 0)),
                   pl.BlockSpec((1, ts, LANES), lambda bi, ti: (bi, ti, 0))),
        scratch_shapes=[pltpu.VMEM((SUBLANES, w), F32),
                        pltpu.VMEM((1, w), F32),
                        pltpu.VMEM((1, LANES), F32)],
        compiler_params=pltpu.CompilerParams(
            dimension_semantics=("arbitrary", "arbitrary"), vmem_limit_bytes=VMEM_LIMIT),
        name="lru",
    )(gr, f, cw, cb, wa, ba, wx, bx, lam, bfg)


def _attn_kernel(q_ref, k_ref, v_ref, cs_ref, o_ref, ka_e, ka_o, va_e, va_o, *, t):
    head0 = 2 * pl.program_id(1)
    i = pl.program_id(2)
    lane = lax.broadcasted_iota(jnp.int32, (1, LANES), 1)
    lo = lane < HEAD_DIM
    den_e = HEAD_DIM

    def ones_where(cond, rows):
        return jnp.broadcast_to(jnp.where(cond, 1.0, 0.0), (rows, LANES)).astype(BF16)

    @pl.when(i == 0)
    def _():
        r = lax.broadcasted_iota(jnp.int32, (LANES, LANES), 0)
        c = lax.broadcasted_iota(jnp.int32, (LANES, LANES), 1)

        def pieces(head, base):
            j = c - base
            sel = jnp.where((j >= 0) & (j < DECAY_PARTS) & (r == head + ATT_HEADS * j), 1.0, 0.0)
            return jnp.dot(cs_ref[0], sel.astype(BF16), preferred_element_type=F32).astype(BF16)

        k2 = k_ref[0]
        v2 = v_ref[0]
        ka_e[...] = jnp.where(lo, k2, pieces(head0, HEAD_DIM))
        ka_o[...] = jnp.where(lo, pieces(head0 + 1, 0), k2)
        va_e[...] = jnp.where(lo, v2, ones_where(lane == den_e, v2.shape[0]))
        va_o[...] = jnp.where(lo, ones_where(lane == 0, v2.shape[0]), v2)

    q = q_ref[0]
    qa_e = jnp.where(lo, q, ones_where(lane < HEAD_DIM + DECAY_PARTS, t))
    qa_o = jnp.where(lo, ones_where(lane < DECAY_PARTS, t), q)
    nt = (((1,), (1,)), ((), ()))

    def block(j, carry, masked):
        start = pl.multiple_of(j * t, t)
        if masked:
            rr = lax.broadcasted_iota(jnp.int32, (t, t), 0)
            cc = lax.broadcasted_iota(jnp.int32, (t, t), 1)
            keep = cc <= rr

        def head(qa, ka, va, m, acc):
            s = lax.dot_general(qa, ka[pl.ds(start, t), :], nt, preferred_element_type=F32)
            if masked:
                s = jnp.where(keep, s, MASK_VALUE)
            m_new = jnp.maximum(m, jnp.max(s, axis=-1, keepdims=True))
            p = jnp.exp2(s - m_new).astype(BF16)
            acc = acc * jnp.exp2(m - m_new) + jnp.dot(p, va[pl.ds(start, t), :],
                                                     preferred_element_type=F32)
            return m_new, acc

        m_e, acc_e, m_o, acc_o = carry
        m_e, acc_e = head(qa_e, ka_e, va_e, m_e, acc_e)
        m_o, acc_o = head(qa_o, ka_o, va_o, m_o, acc_o)
        return m_e, acc_e, m_o, acc_o

    m0 = jnp.full((t, 1), MASK_VALUE, F32)
    a0 = jnp.zeros((t, LANES), F32)
    carry = lax.fori_loop(0, i, lambda j, c: block(j, c, False), (m0, a0, m0, a0))
    _, acc_e, _, acc_o = block(i, carry, True)
    o_ref[0] = jnp.where(lo, acc_e / acc_e[:, den_e:den_e + 1], acc_o / acc_o[:, 0:1]).astype(BF16)


def _attn(qkv, cs, t):
    b, s, _ = qkv.shape
    hp = ATT_WIDTH // LANES
    return pl.pallas_call(
        functools.partial(_attn_kernel, t=t),
        out_shape=jax.ShapeDtypeStruct((b, s, ATT_WIDTH), BF16),
        grid=(b, hp, s // t),
        in_specs=[pl.BlockSpec((1, t, LANES), lambda bi, h, i: (bi, i, h)),
                  pl.BlockSpec((1, s, LANES), lambda bi, h, i: (bi, 0, hp + h)),
                  pl.BlockSpec((1, s, LANES), lambda bi, h, i: (bi, 0, 2 * hp + h)),
                  pl.BlockSpec((1, s, LANES), lambda bi, h, i: (bi, 0, 0))],
        out_specs=pl.BlockSpec((1, t, LANES), lambda bi, h, i: (bi, i, h)),
        scratch_shapes=[pltpu.VMEM((s, LANES), BF16)] * 4,
        compiler_params=pltpu.CompilerParams(
            dimension_semantics=("arbitrary", "arbitrary", "arbitrary"),
            vmem_limit_bytes=VMEM_LIMIT),
        name="attn",
    )(qkv, qkv, qkv, cs)


def _out_route_kernel(x_ref, att_ref, lru_ref, wo_ref, g_ref, wr_ref, br_ref,
                      x1_ref, xs_ref, tok_ref, nch_ref):
    tm = x_ref.shape[0]
    aw = att_ref.shape[1]
    x1 = (x_ref[...]
          + jnp.dot(att_ref[...], wo_ref[:aw, :], preferred_element_type=F32)
          + jnp.dot(lru_ref[...], wo_ref[aw:, :], preferred_element_type=F32))
    x1_ref[...] = x1
    h2 = _rms(x1, g_ref[...])
    h2_hi = h2.astype(BF16)
    h2_lo = (h2 - h2_hi.astype(F32)).astype(BF16)

    logits = (jnp.dot(h2_hi, wr_ref[0], preferred_element_type=F32)
              + jnp.dot(h2_lo, wr_ref[0], preferred_element_type=F32)
              + jnp.dot(h2_hi, wr_ref[1], preferred_element_type=F32)) + br_ref[...]
    lane = lax.broadcasted_iota(jnp.int32, (tm, LANES), 1)
    gl = jnp.where(lane < N_GROUPS, logits, MASK_VALUE)
    gmax = jnp.max(gl, axis=-1, keepdims=True)
    gsum = jnp.sum(jnp.exp(gl - gmax), axis=-1, keepdims=True)
    g_w = 1.0 / gsum
    g_idx = jnp.min(jnp.where(gl == gmax, lane, LANES), axis=-1, keepdims=True)
    in_group = (lane >= N_GROUPS) & (lane < N_GROUPS + N_EXPERTS) & (
        lax.shift_right_logical(lane - N_GROUPS, 3) == g_idx)
    il = jnp.where(in_group, logits, MASK_VALUE)
    m1 = jnp.max(il, axis=-1, keepdims=True)
    e1 = jnp.min(jnp.where(il == m1, lane, LANES), axis=-1, keepdims=True)
    il2 = jnp.where(lane == e1, MASK_VALUE, il)
    m2 = jnp.max(il2, axis=-1, keepdims=True)
    e2 = jnp.min(jnp.where(il2 == m2, lane, LANES), axis=-1, keepdims=True)
    ratio = jnp.exp(m2 - m1)
    w1 = g_w / (1.0 + ratio)
    w2 = w1 * ratio
    hot1 = lane == e1 - N_GROUPS
    hot2 = lane == e2 - N_GROUPS

    onehot = jnp.where(hot1 | hot2, 1.0, 0.0)
    rr = lax.broadcasted_iota(jnp.int32, (tm, tm), 0)
    cc = lax.broadcasted_iota(jnp.int32, (tm, tm), 1)
    tri = jnp.where(cc < rr, 1.0, 0.0).astype(BF16)
    before = jnp.dot(tri, onehot.astype(BF16), preferred_element_type=F32)
    cnt = jnp.sum(onehot, axis=0, keepdims=True)
    nch = jnp.floor((cnt + (CHUNK - 1)) * (1.0 / CHUNK))
    r128 = lax.broadcasted_iota(jnp.int32, (LANES, LANES), 0)
    c128 = lax.broadcasted_iota(jnp.int32, (LANES, LANES), 1)
    upper = jnp.where(r128 < c128, 1.0, 0.0).astype(BF16)
    nch8 = jnp.broadcast_to(nch, (SUBLANES, LANES))
    seg_off = CHUNK * jnp.dot(nch8.astype(BF16), upper, preferred_element_type=F32)[0:1, :]
    pos = seg_off + before
    pos1 = jnp.sum(jnp.where(hot1, pos, 0.0), axis=-1, keepdims=True)
    pos2 = jnp.sum(jnp.where(hot2, pos, 0.0), axis=-1, keepdims=True)
    nch_ref[0] = nch8

    tok = jnp.where(lane == 0, pos1, 0.0)
    tok = jnp.where(lane == 1, pos2, tok)
    tok = jnp.where(lane == 2, w1, tok)
    tok = jnp.where(lane == 3, w2, tok)
    tok_ref[...] = tok

    tok_t = jnp.transpose(tok)
    p1 = tok_t[0:1, :].astype(jnp.int32)
    p2 = tok_t[1:2, :].astype(jnp.int32)
    rows = xs_ref.shape[0]
    ri = lax.broadcasted_iota(jnp.int32, (rows, tm), 0)
    perm = jnp.where((ri == p1) | (ri == p2), 1.0, 0.0).astype(BF16)
    xs_ref[...] = jnp.dot(perm, h2_hi, preferred_element_type=F32).astype(BF16)


def _out_route(x2, att, lru, wo, g, wr, br, tm):
    n, d = x2.shape
    aw = att.shape[1]
    nblk = n // tm
    rows = _block_rows(tm)
    row = lambda c: pl.BlockSpec((tm, c), lambda i: (i, 0))
    full = lambda shape: pl.BlockSpec(shape, lambda i: (0,) * len(shape))
    return pl.pallas_call(
        _out_route_kernel,
        out_shape=(jax.ShapeDtypeStruct((n, d), F32),
                   jax.ShapeDtypeStruct((nblk * rows, d), BF16),
                   jax.ShapeDtypeStruct((n, LANES), F32),
                   jax.ShapeDtypeStruct((nblk, SUBLANES, LANES), F32)),
        grid=(nblk,),
        in_specs=[row(d), row(aw), row(d - aw), full((d, d)), full((1, d)),
                  full((2, d, LANES)), full((1, LANES))],
        out_specs=(row(d), pl.BlockSpec((rows, d), lambda i: (i, 0)), row(LANES),
                   pl.BlockSpec((1, SUBLANES, LANES), lambda i: (i, 0, 0))),
        compiler_params=pltpu.CompilerParams(
            dimension_semantics=("arbitrary",), vmem_limit_bytes=VMEM_LIMIT),
        name="out_route",
    )(x2, att, lru, wo, g, wr, br)


def _experts_kernel(te_ref, nu_ref, src_ref, nv_ref, xs_hbm, wg_ref, wu_ref, wd_ref, init_hbm,
                    ys_hbm, xbuf, ybuf, gsem, ssem):
    del te_ref, init_hbm
    i = pl.program_id(0)
    nu = nu_ref[0]

    def gather(tile, k):
        row = pl.multiple_of(src_ref[tile * TILE_CHUNKS + k], CHUNK)
        slot = tile % 2
        return pltpu.make_async_copy(xs_hbm.at[pl.ds(row, CHUNK), :],
                                     xbuf.at[slot, pl.ds(k * CHUNK, CHUNK), :], gsem.at[slot])

    def scatter(tile, k):
        row = pl.multiple_of(src_ref[tile * TILE_CHUNKS + k], CHUNK)
        slot = tile % 2
        return pltpu.make_async_copy(ybuf.at[slot, pl.ds(k * CHUNK, CHUNK), :],
                                     ys_hbm.at[pl.ds(row, CHUNK), :], ssem.at[slot])

    def start_gathers(tile):
        for k in range(TILE_CHUNKS):
            gather(tile, k).start()

    def wait_scatters(tile):
        for k in range(TILE_CHUNKS):
            @pl.when(k < nv_ref[tile])
            def _():
                scatter(tile, k).wait()

    @pl.when((i == 0) & (nu > 0))
    def _():
        start_gathers(0)

    @pl.when(i + 1 < nu)
    def _():
        start_gathers(i + 1)

    @pl.when(i < nu)
    def _():
        for k in range(TILE_CHUNKS):
            gather(i, k).wait()
        slot = i % 2
        x = xbuf[slot]
        g = jnp.dot(x, wg_ref[0], preferred_element_type=F32)
        u = jnp.dot(x, wu_ref[0], preferred_element_type=F32)
        h = (jax.nn.silu(g) * u).astype(BF16)
        ybuf[slot] = jnp.dot(h, wd_ref[0], preferred_element_type=F32).astype(BF16)
        for k in range(TILE_CHUNKS):
            @pl.when(k < nv_ref[i])
            def _():
                scatter(i, k).start()

        @pl.when(i > 0)
        def _():
            wait_scatters(i - 1)

        @pl.when(i == nu - 1)
        def _():
            wait_scatters(i)


def _experts(tile_expert, n_used, chunk_src, n_valid, xs, wg, wu, wd):
    rows, d = xs.shape
    f = wg.shape[2]
    tm = TILE_CHUNKS * CHUNK
    n_tiles = tile_expert.shape[0]
    wspec = lambda shape: pl.BlockSpec(shape, lambda i, te, nu, src, nv: (te[i], 0, 0))
    grid_spec = pltpu.PrefetchScalarGridSpec(
        num_scalar_prefetch=4,
        grid=(n_tiles,),
        in_specs=[pl.BlockSpec(memory_space=pl.ANY),
                  wspec((1, d, f)), wspec((1, d, f)), wspec((1, f, d)),
                  pl.BlockSpec(memory_space=pl.ANY)],
        out_specs=pl.BlockSpec(memory_space=pl.ANY),
        scratch_shapes=[pltpu.VMEM((2, tm, d), BF16), pltpu.VMEM((2, tm, d), BF16),
                        pltpu.SemaphoreType.DMA((2,)), pltpu.SemaphoreType.DMA((2,))],
    )
    return pl.pallas_call(
        _experts_kernel,
        out_shape=jax.ShapeDtypeStruct((rows, d), BF16),
        grid_spec=grid_spec,
        input_output_aliases={8: 0},
        compiler_params=pltpu.CompilerParams(
            dimension_semantics=("arbitrary",), vmem_limit_bytes=VMEM_LIMIT),
        name="experts",
    )(tile_expert, n_used, chunk_src, n_valid, xs, wg, wu, wd, jnp.zeros((rows, d), BF16))


def _combine_kernel(x1_ref, tok_ref, g_ref, ys_ref, o_ref):
    tm = x1_ref.shape[0]
    rows = ys_ref.shape[0]
    tok = tok_ref[...]
    pos1 = tok[:, 0:1].astype(jnp.int32)
    pos2 = tok[:, 1:2].astype(jnp.int32)
    ci = lax.broadcasted_iota(jnp.int32, (tm, rows), 1)
    unperm = jnp.where(ci == pos1, tok[:, 2:3], jnp.where(ci == pos2, tok[:, 3:4], 0.0))
    y = jnp.dot(unperm.astype(BF16), ys_ref[...], preferred_element_type=F32)
    o_ref[...] = _rms(x1_ref[...] + y, g_ref[...])


def _combine(x1, tok, g, ys, tm):
    n, d = x1.shape
    rows = _block_rows(tm)
    return pl.pallas_call(
        _combine_kernel,
        out_shape=jax.ShapeDtypeStruct((n, d), F32),
        grid=(n // tm,),
        in_specs=[pl.BlockSpec((tm, d), lambda i: (i, 0)),
                  pl.BlockSpec((tm, LANES), lambda i: (i, 0)),
                  pl.BlockSpec((1, d), lambda i: (0, 0)),
                  pl.BlockSpec((rows, d), lambda i: (i, 0))],
        out_specs=pl.BlockSpec((tm, d), lambda i: (i, 0)),
        compiler_params=pltpu.CompilerParams(
            dimension_semantics=("arbitrary",), vmem_limit_bytes=VMEM_LIMIT),
        name="combine",
    )(x1, tok, g, ys)


def _block_diag(w):
    nb, bd, _ = w.shape
    eye = jnp.eye(nb, dtype=w.dtype)
    return (eye[:, None, :, None] * w[:, :, None, :]).reshape(nb * bd, nb * bd)


def _tile_tables(nch, rows):
    nblk = nch.shape[0]
    ex_cum = lambda a, axis: jnp.cumsum(a, axis=axis) - a
    ch_off = ex_cum(nch, 1)
    blk_cum = ex_cum(nch, 0)
    tot = jnp.sum(nch, axis=0)
    tiles_e = (tot + TILE_CHUNKS - 1) // TILE_CHUNKS
    t_end = jnp.cumsum(tiles_e)
    t_start = t_end - tiles_e
    max_chunks = nblk * ((rows - N_EXPERTS * CHUNK) // CHUNK + N_EXPERTS)
    n_tiles = pl.cdiv(max_chunks, TILE_CHUNKS) + N_EXPERTS
    tid = jnp.arange(n_tiles, dtype=jnp.int32)
    te = jnp.minimum(jnp.sum((tid[:, None] >= t_end[None, :]).astype(jnp.int32), axis=1),
                     N_EXPERTS - 1)
    q = (tid - t_start[te])[:, None] * TILE_CHUNKS + jnp.arange(TILE_CHUNKS, dtype=jnp.int32)[None, :]
    valid = (tid < t_end[-1])[:, None] & (q < tot[te][:, None])
    seg_start = blk_cum.T[te]
    seg_end = seg_start + nch.T[te]
    bidx = jnp.minimum(jnp.sum((q[:, :, None] >= seg_end[:, None, :]).astype(jnp.int32), axis=-1),
                       nblk - 1)
    start_b = jnp.take_along_axis(seg_start, bidx, axis=1)
    off_b = jnp.take_along_axis(ch_off.T[te], bidx, axis=1)
    src = bidx * rows + CHUNK * (off_b + q - start_b)
    src = jnp.where(valid, src, rows - CHUNK).astype(jnp.int32)
    n_valid = jnp.sum(valid.astype(jnp.int32), axis=1)
    return te, t_end[-1:].astype(jnp.int32), src.reshape(-1), n_valid


def kernel(x, mix_norm, w_in, b_forget, conv_w, conv_b, w_a, b_a, w_x, b_x, lru_lambda, w_out,
           ffn_norm, w_group, b_group, w_inner, b_inner, w_gate, w_up, w_down, final_norm):
    b, s, d = x.shape
    assert w_in.shape[0] == 1, "the final rmsnorm is fused into the single layer's combine stage"
    n = b * s
    tm = min(512, s)
    x2 = x.reshape(n, d)

    sizes = (ATT_WIDTH, ATT_WIDTH, ATT_WIDTH, ATT_HEADS, LRU_WIDTH, LRU_WIDTH)
    o = [0]
    for v in sizes:
        o.append(o[-1] + v)
    w_f = jnp.pad(w_in[0][:, o[3]:o[4]], ((0, 0), (0, LANES - ATT_HEADS)))
    w_cat = jnp.concatenate([w_in[0][:, :o[3]], w_in[0][:, o[4]:], w_f], axis=1).astype(BF16)
    qkv, gr, f = _in_proj(x2, mix_norm.reshape(1, d), w_cat, tm)

    bfg = jnp.pad(b_forget[0], (0, LANES - ATT_HEADS)).reshape(1, LANES)
    lru, cs = _lru(gr.reshape(b, s, -1), f.reshape(b, s, LANES),
                   conv_w[0], conv_b.reshape(1, -1),
                   _block_diag(w_a[0]).astype(BF16), b_a.reshape(1, -1),
                   _block_diag(w_x[0]).astype(BF16), b_x.reshape(1, -1),
                   lru_lambda.reshape(1, -1), bfg, tm)

    att = _attn(qkv.reshape(b, s, -1), cs, tm)

    wr = jnp.concatenate(
        [w_group[0], jnp.transpose(w_inner[0], (1, 0, 2)).reshape(d, N_EXPERTS)], axis=1)
    wr = jnp.pad(wr, ((0, 0), (0, LANES - wr.shape[1])))
    wr_hi = wr.astype(BF16)
    wr = jnp.stack([wr_hi, (wr - wr_hi.astype(F32)).astype(BF16)])
    br = jnp.concatenate([b_group[0], b_inner[0].reshape(-1)])
    br = jnp.pad(br, (0, LANES - br.shape[0])).reshape(1, LANES)
    x1, xs, tok, nch = _out_route(x2, att.reshape(n, -1), lru.reshape(n, -1),
                                  w_out[0].astype(BF16), ffn_norm.reshape(1, d), wr, br, tm)

    tables = _tile_tables(nch[:, 0, :N_EXPERTS].astype(jnp.int32), _block_rows(tm))
    fe = w_gate.shape[-1]
    ys = _experts(*tables, xs,
                  w_gate.reshape(N_EXPERTS, d, fe).astype(BF16),
                  w_up.reshape(N_EXPERTS, d, fe).astype(BF16),
                  w_down.reshape(N_EXPERTS, fe, d).astype(BF16))
    out = _combine(x1, tok, final_norm.reshape(1, d), ys, tm)
    return out.reshape(b, s, d)
```

```python
import functools

import jax
import jax.numpy as jnp
from jax import lax
from jax.experimental import pallas as pl
from jax.experimental.pallas import tpu as pltpu

ATT_HEADS = 8
HEAD_DIM = 64
ATT_WIDTH = ATT_HEADS * HEAD_DIM
LRU_WIDTH = 512
LRU_BLOCKS = 8
CONV_WIDTH = 4
LRU_C = 8.0
N_GROUPS = 4
EXPERTS_PER_GROUP = 8
N_EXPERTS = N_GROUPS * EXPERTS_PER_GROUP
D_EXPERT = 256
RMS_EPS = 1e-6
LANES = 128
SUBLANES = 8
CHUNK = 16
TILE_CHUNKS = 16
MASK_VALUE = -1e30
LOG2E = 1.4426950408889634
DECAY_PARTS = 3
VMEM_LIMIT = 48 * 1024 * 1024

BF16 = jnp.bfloat16
F32 = jnp.float32


def _rms(x, g):
    inv = lax.rsqrt(jnp.mean(x * x, axis=-1, keepdims=True) + RMS_EPS)
    return x * inv * g


def _block_rows(tm):
    return 2 * tm + N_EXPERTS * CHUNK


def _in_proj_kernel(x_ref, g_ref, w_ref, qkv_ref, gr_ref, f_ref):
    h = _rms(x_ref[...], g_ref[...]).astype(BF16)
    nq = qkv_ref.shape[1]
    ng = gr_ref.shape[1]
    aw = ATT_WIDTH
    q = jnp.dot(h, w_ref[:, :aw], preferred_element_type=F32) * (HEAD_DIM ** -0.5 * LOG2E)
    qkv_ref[:, :aw] = q.astype(BF16)
    qkv_ref[:, aw:] = jnp.dot(h, w_ref[:, aw:nq], preferred_element_type=F32).astype(BF16)
    gr_ref[...] = jnp.dot(h, w_ref[:, nq:nq + ng], preferred_element_type=F32)
    f_ref[...] = jnp.dot(h, w_ref[:, nq + ng:], preferred_element_type=F32)


def _in_proj(x2, g, w, tm):
    n, d = x2.shape
    nq, ng = 3 * ATT_WIDTH, 2 * LRU_WIDTH
    return pl.pallas_call(
        _in_proj_kernel,
        out_shape=(jax.ShapeDtypeStruct((n, nq), BF16),
                   jax.ShapeDtypeStruct((n, ng), F32),
                   jax.ShapeDtypeStruct((n, LANES), F32)),
        grid=(n // tm,),
        in_specs=[pl.BlockSpec((tm, d), lambda i: (i, 0)),
                  pl.BlockSpec((1, d), lambda i: (0, 0)),
                  pl.BlockSpec(w.shape, lambda i: (0, 0))],
        out_specs=(pl.BlockSpec((tm, nq), lambda i: (i, 0)),
                   pl.BlockSpec((tm, ng), lambda i: (i, 0)),
                   pl.BlockSpec((tm, LANES), lambda i: (i, 0))),
        compiler_params=pltpu.CompilerParams(
            dimension_semantics=("arbitrary",), vmem_limit_bytes=VMEM_LIMIT),
        name="in_proj",
    )(x2, g, w)


def _shift_rows(x, d, fill, row):
    return jnp.where(row < d, fill, pltpu.roll(x, d, 0))


def _lru_kernel(gr_ref, f_ref, cw_ref, cb_ref, wa_ref, ba_ref, wx_ref, bx_ref, lam_ref, bf_ref,
                lru_ref, cs_ref, xprev, hprev, cprev):
    @pl.when(pl.program_id(1) == 0)
    def _():
        xprev[...] = jnp.zeros_like(xprev)
        hprev[...] = jnp.zeros_like(hprev)
        cprev[...] = jnp.zeros_like(cprev)

    ts = gr_ref.shape[1]
    w = LRU_WIDTH
    gate = gr_ref[0, :, :w]
    rec = gr_ref[0, :, w:]
    row = lax.broadcasted_iota(jnp.int32, (ts, w), 0)
    row8 = lax.broadcasted_iota(jnp.int32, (SUBLANES, w), 0)

    prev = xprev[...]
    conv = cb_ref[...] + cw_ref[CONV_WIDTH - 1:CONV_WIDTH, :] * rec
    for k in range(1, CONV_WIDTH):
        rolled = pltpu.roll(rec, k, 0)
        head = jnp.where(row8 < k, pltpu.roll(prev, k, 0), rolled[:SUBLANES])
        shifted = jnp.concatenate([head, rolled[SUBLANES:]], axis=0)
        conv = conv + cw_ref[CONV_WIDTH - 1 - k:CONV_WIDTH - k, :] * shifted
    xprev[...] = rec[ts - SUBLANES:, :]

    cb16 = conv.astype(BF16)
    r = jax.nn.sigmoid(jnp.dot(cb16, wa_ref[...], preferred_element_type=F32) + ba_ref[...])
    i = jax.nn.sigmoid(jnp.dot(cb16, wx_ref[...], preferred_element_type=F32) + bx_ref[...])
    log_a = (-LRU_C) * r * jax.nn.softplus(-lam_ref[...])
    a = jnp.exp(log_a)
    th = jnp.tanh(log_a)
    u = jnp.sqrt(-2.0 * th / (1.0 - th)) * (i * conv)

    d = 1
    while d < ts:
        u = a * _shift_rows(u, d, 0.0, row) + u
        a = a * _shift_rows(a, d, 1.0, row)
        d *= 2
    h = u + a * hprev[...]
    hprev[...] = h[ts - 1:, :]
    lru_ref[0] = (h * jax.nn.gelu(gate, approximate=True)).astype(BF16)

    c = jax.nn.log_sigmoid(f_ref[0] + bf_ref[...])
    rowc = lax.broadcasted_iota(jnp.int32, (ts, LANES), 0)
    lanec = lax.broadcasted_iota(jnp.int32, (ts, LANES), 1)
    d = 1
    while d < ts:
        c = c + _shift_rows(c, d, 0.0, rowc)
        d *= 2
    c = c + cprev[...]
    cprev[...] = c[ts - 1:, :]
    rest = c * (-LOG2E)
    parts = jnp.zeros_like(rest)
    for j in range(DECAY_PARTS):
        piece = rest.astype(BF16).astype(F32)
        rest = rest - piece
        shifted = piece if j == 0 else pltpu.roll(piece, ATT_HEADS * j, 1)
        parts = jnp.where((lanec >= ATT_HEADS * j) & (lanec < ATT_HEADS * (j + 1)), shifted, parts)
    cs_ref[0] = parts.astype(BF16)


def _lru(gr, f, cw, cb, wa, ba, wx, bx, lam, bfg, ts):
    b, s, _ = gr.shape
    w = LRU_WIDTH
    full = lambda shape: pl.BlockSpec(shape, lambda bi, ti: (0,) * len(shape))
    return pl.pallas_call(
        _lru_kernel,
        out_shape=(jax.ShapeDtypeStruct((b, s, w), BF16),
                   jax.ShapeDtypeStruct((b, s, LANES), BF16)),
        grid=(b, s // ts),
        in_specs=[pl.BlockSpec((1, ts, 2 * w), lambda bi, ti: (bi, ti, 0)),
                  pl.BlockSpec((1, ts, LANES), lambda bi, ti: (bi, ti, 0)),
                  full((CONV_WIDTH, w)), full((1, w)),
                  full((w, w)), full((1, w)), full((w, w)), full((1, w)),
                  full((1, w)), full((1, LANES))],
        out_specs=(pl.BlockSpec((1, ts, w), lambda bi, ti: (bi, ti, 0)),
                   pl.BlockSpec((1, ts, LANES), lambda bi, ti: (bi, ti, 0))),
        scratch_shapes=[pltpu.VMEM((SUBLANES, w), F32),
                        pltpu.VMEM((1, w), F32),
                        pltpu.VMEM((1, LANES), F32)],
        compiler_params=pltpu.CompilerParams(
            dimension_semantics=("arbitrary", "arbitrary"), vmem_limit_bytes=VMEM_LIMIT),
        name="lru",
    )(gr, f, cw, cb, wa, ba, wx, bx, lam, bfg)


def _attn_kernel(q_ref, k_ref, v_ref, cs_ref, o_ref, ka_e, ka_o, va_e, va_o,
                 m_e, m_o, acc_e, acc_o, *, t):
    head0 = 2 * pl.program_id(1)
    i = pl.program_id(2)
    lane = lax.broadcasted_iota(jnp.int32, (1, LANES), 1)
    lo = lane < HEAD_DIM
    den_e = HEAD_DIM

    def ones_where(cond, rows):
        return jnp.broadcast_to(jnp.where(cond, 1.0, 0.0), (rows, LANES)).astype(BF16)

    @pl.when(i == 0)
    def _():
        r = lax.broadcasted_iota(jnp.int32, (LANES, LANES), 0)
        c = lax.broadcasted_iota(jnp.int32, (LANES, LANES), 1)

        def pieces(head, base):
            j = c - base
            sel = jnp.where((j >= 0) & (j < DECAY_PARTS) & (r == head + ATT_HEADS * j), 1.0, 0.0)
            return jnp.dot(cs_ref[0], sel.astype(BF16), preferred_element_type=F32).astype(BF16)

        k2 = k_ref[0]
        v2 = v_ref[0]
        ka_e[...] = jnp.where(lo, k2, pieces(head0, HEAD_DIM))
        ka_o[...] = jnp.where(lo, pieces(head0 + 1, 0), k2)
        va_e[...] = jnp.where(lo, v2, ones_where(lane == den_e, v2.shape[0]))
        va_o[...] = jnp.where(lo, ones_where(lane == 0, v2.shape[0]), v2)

    q = q_ref[0]
    qa_e = jnp.where(lo, q, ones_where(lane < HEAD_DIM + DECAY_PARTS, t))
    qa_o = jnp.where(lo, ones_where(lane < DECAY_PARTS, t), q)
    nt = (((1,), (1,)), ((), ()))

    heads = ((qa_e, ka_e, va_e, m_e, acc_e), (qa_o, ka_o, va_o, m_o, acc_o))
    for _, _, _, m_ref, acc_ref in heads:
        m_ref[...] = jnp.full(m_ref.shape, MASK_VALUE, F32)
        acc_ref[...] = jnp.zeros(acc_ref.shape, F32)

    def block(start, width, first_masked_col):
        if not isinstance(start, int):
            start = pl.multiple_of(start, t)
        if first_masked_col is not None:
            rr = lax.broadcasted_iota(jnp.int32, (t, width), 0)
            cc = lax.broadcasted_iota(jnp.int32, (t, width), 1)
            keep = cc - first_masked_col <= rr
        updates = []
        for qa, ka, va, m_ref, acc_ref in heads:
            s = lax.dot_general(qa, ka[pl.ds(start, width), :], nt, preferred_element_type=F32)
            if first_masked_col is not None:
                s = jnp.where(keep, s, MASK_VALUE)
            cols = [s[:, c * LANES:(c + 1) * LANES] for c in range(width // LANES)]
            m_old = m_ref[...]
            row_max = jnp.max(functools.reduce(jnp.maximum, cols), axis=-1, keepdims=True)
            m_new = jnp.maximum(m_old, row_max)
            p = jnp.concatenate([jnp.exp2(c - m_new) for c in cols], axis=1).astype(BF16)
            acc = acc_ref[...] * jnp.exp2(m_old - m_new) + jnp.dot(
                p, va[pl.ds(start, width), :], preferred_element_type=F32)
            updates.append((m_ref, m_new, acc_ref, acc))
        for m_ref, m_new, acc_ref, acc in updates:
            m_ref[...] = m_new
            acc_ref[...] = acc

    before = jnp.maximum(i - 1, 0)

    def double(jj, c):
        block(jj * (2 * t), 2 * t, None)
        return c

    lax.fori_loop(0, before // 2, double, 0)

    @pl.when(before % 2 == 1)
    def _():
        block((before - 1) * t, t, None)

    @pl.when(i > 0)
    def _():
        block((i - 1) * t, 2 * t, t)

    @pl.when(i == 0)
    def _():
        block(0, t, 0)

    a_e = acc_e[...]
    a_o = acc_o[...]
    o_ref[0] = jnp.where(lo, a_e / a_e[:, den_e:den_e + 1], a_o / a_o[:, 0:1]).astype(BF16)


def _attn(qkv, cs, t):
    b, s, _ = qkv.shape
    hp = ATT_WIDTH // LANES
    return pl.pallas_call(
        functools.partial(_attn_kernel, t=t),
        out_shape=jax.ShapeDtypeStruct((b, s, ATT_WIDTH), BF16),
        grid=(b, hp, s // t),
        in_specs=[pl.BlockSpec((1, t, LANES), lambda bi, h, i: (bi, i, h)),
                  pl.BlockSpec((1, s, LANES), lambda bi, h, i: (bi, 0, hp + h)),
                  pl.BlockSpec((1, s, LANES), lambda bi, h, i: (bi, 0, 2 * hp + h)),
                  pl.BlockSpec((1, s, LANES), lambda bi, h, i: (bi, 0, 0))],
        out_specs=pl.BlockSpec((1, t, LANES), lambda bi, h, i: (bi, i, h)),
        scratch_shapes=([pltpu.VMEM((s, LANES), BF16)] * 4 + [pltpu.VMEM((t, LANES), F32)] * 2
                        + [pltpu.VMEM((t, LANES), F32)] * 2),
        compiler_params=pltpu.CompilerParams(
            dimension_semantics=("arbitrary", "arbitrary", "arbitrary"),
            vmem_limit_bytes=VMEM_LIMIT),
        name="attn",
    )(qkv, qkv, qkv, cs)


def _out_route_kernel(x_ref, att_ref, lru_ref, wo_ref, g_ref, wr_ref, br_ref,
                      x1_ref, xs_ref, tok_ref, nch_ref):
    tm = x_ref.shape[0]
    aw = att_ref.shape[1]
    x1 = (x_ref[...]
          + jnp.dot(att_ref[...], wo_ref[:aw, :], preferred_element_type=F32)
          + jnp.dot(lru_ref[...], wo_ref[aw:, :], preferred_element_type=F32))
    x1_ref[...] = x1
    h2 = _rms(x1, g_ref[...])
    h2_hi = h2.astype(BF16)
    h2_lo = (h2 - h2_hi.astype(F32)).astype(BF16)

    logits = (jnp.dot(h2_hi, wr_ref[0], preferred_element_type=F32)
              + jnp.dot(h2_lo, wr_ref[0], preferred_element_type=F32)
              + jnp.dot(h2_hi, wr_ref[1], preferred_element_type=F32)) + br_ref[...]
    lane = lax.broadcasted_iota(jnp.int32, (tm, LANES), 1)
    gl = jnp.where(lane < N_GROUPS, logits, MASK_VALUE)
    gmax = jnp.max(gl, axis=-1, keepdims=True)
    gsum = jnp.sum(jnp.exp(gl - gmax), axis=-1, keepdims=True)
    g_w = 1.0 / gsum
    g_idx = jnp.min(jnp.where(gl == gmax, lane, LANES), axis=-1, keepdims=True)
    in_group = (lane >= N_GROUPS) & (lane < N_GROUPS + N_EXPERTS) & (
        lax.shift_right_logical(lane - N_GROUPS, 3) == g_idx)
    il = jnp.where(in_group, logits, MASK_VALUE)
    m1 = jnp.max(il, axis=-1, keepdims=True)
    e1 = jnp.min(jnp.where(il == m1, lane, LANES), axis=-1, keepdims=True)
    il2 = jnp.where(lane == e1, MASK_VALUE, il)
    m2 = jnp.max(il2, axis=-1, keepdims=True)
    e2 = jnp.min(jnp.where(il2 == m2, lane, LANES), axis=-1, keepdims=True)
    ratio = jnp.exp(m2 - m1)
    w1 = g_w / (1.0 + ratio)
    w2 = w1 * ratio
    hot1 = lane == e1 - N_GROUPS
    hot2 = lane == e2 - N_GROUPS

    onehot = jnp.where(hot1 | hot2, 1.0, 0.0)
    rr = lax.broadcasted_iota(jnp.int32, (tm, tm), 0)
    cc = lax.broadcasted_iota(jnp.int32, (tm, tm), 1)
    tri = jnp.where(cc < rr, 1.0, 0.0).astype(BF16)
    before = jnp.dot(tri, onehot.astype(BF16), preferred_element_type=F32)
    cnt = jnp.sum(onehot, axis=0, keepdims=True)
    nch = jnp.floor((cnt + (CHUNK - 1)) * (1.0 / CHUNK))
    r128 = lax.broadcasted_iota(jnp.int32, (LANES, LANES), 0)
    c128 = lax.broadcasted_iota(jnp.int32, (LANES, LANES), 1)
    upper = jnp.where(r128 < c128, 1.0, 0.0).astype(BF16)
    nch8 = jnp.broadcast_to(nch, (SUBLANES, LANES))
    seg_off = CHUNK * jnp.dot(nch8.astype(BF16), upper, preferred_element_type=F32)[0:1, :]
    pos = seg_off + before
    pos1 = jnp.sum(jnp.where(hot1, pos, 0.0), axis=-1, keepdims=True)
    pos2 = jnp.sum(jnp.where(hot2, pos, 0.0), axis=-1, keepdims=True)
    nch_ref[0] = nch8

    tok = jnp.where(lane == 0, pos1, 0.0)
    tok = jnp.where(lane == 1, pos2, tok)
    tok = jnp.where(lane == 2, w1, tok)
    tok = jnp.where(lane == 3, w2, tok)
    tok_ref[...] = tok

    tok_t = jnp.transpose(tok)
    p1 = tok_t[0:1, :].astype(jnp.int32)
    p2 = tok_t[1:2, :].astype(jnp.int32)
    rows = xs_ref.shape[0]
    ri = lax.broadcasted_iota(jnp.int32, (rows, tm), 0)
    perm = jnp.where((ri == p1) | (ri == p2), 1.0, 0.0).astype(BF16)
    xs_ref[...] = jnp.dot(perm, h2_hi, preferred_element_type=F32).astype(BF16)


def _out_route(x2, att, lru, wo, g, wr, br, tm):
    n, d = x2.shape
    aw = att.shape[1]
    nblk = n // tm
    rows = _block_rows(tm)
    row = lambda c: pl.BlockSpec((tm, c), lambda i: (i, 0))
    full = lambda shape: pl.BlockSpec(shape, lambda i: (0,) * len(shape))
    return pl.pallas_call(
        _out_route_kernel,
        out_shape=(jax.ShapeDtypeStruct((n, d), F32),
                   jax.ShapeDtypeStruct((nblk * rows, d), BF16),
                   jax.ShapeDtypeStruct((n, LANES), F32),
                   jax.ShapeDtypeStruct((nblk, SUBLANES, LANES), F32)),
        grid=(nblk,),
        in_specs=[row(d), row(aw), row(d - aw), full((d, d)), full((1, d)),
                  full((2, d, LANES)), full((1, LANES))],
        out_specs=(row(d), pl.BlockSpec((rows, d), lambda i: (i, 0)), row(LANES),
                   pl.BlockSpec((1, SUBLANES, LANES), lambda i: (i, 0, 0))),
        compiler_params=pltpu.CompilerParams(
            dimension_semantics=("arbitrary",), vmem_limit_bytes=VMEM_LIMIT),
        name="out_route",
    )(x2, att, lru, wo, g, wr, br)


GATHER_AHEAD = 2
X_SLOTS = GATHER_AHEAD + 1
Y_SLOTS = 2


def _experts_kernel(te_ref, nu_ref, src_ref, dst_ref, xs_hbm, wg_ref, wu_ref, wd_ref, init_hbm,
                    ys_hbm, xbuf, ybuf, gsem, ssem):
    del te_ref, init_hbm
    i = pl.program_id(0)
    nu = nu_ref[0]

    def gather(tile, k):
        row = pl.multiple_of(src_ref[tile * TILE_CHUNKS + k], CHUNK)
        slot = tile % X_SLOTS
        return pltpu.make_async_copy(xs_hbm.at[pl.ds(row, CHUNK), :],
                                     xbuf.at[slot, pl.ds(k * CHUNK, CHUNK), :], gsem.at[slot])

    def scatter(tile, k):
        row = pl.multiple_of(dst_ref[tile * TILE_CHUNKS + k], CHUNK)
        slot = tile % Y_SLOTS
        return pltpu.make_async_copy(ybuf.at[slot, pl.ds(k * CHUNK, CHUNK), :],
                                     ys_hbm.at[pl.ds(row, CHUNK), :], ssem.at[slot])

    def each_chunk(copy, tile, op):
        for k in range(TILE_CHUNKS):
            getattr(copy(tile, k), op)()

    @pl.when((i == 0) & (nu > 0))
    def _():
        for ahead in range(GATHER_AHEAD):
            each_chunk(gather, ahead, "start")

    @pl.when(i < nu)
    def _():
        each_chunk(gather, i, "wait")
        x = xbuf[i % X_SLOTS]
        g = jnp.dot(x, wg_ref[0], preferred_element_type=F32)
        u = jnp.dot(x, wu_ref[0], preferred_element_type=F32)
        h = (jax.nn.silu(g) * u).astype(BF16)
        ybuf[i % Y_SLOTS] = jnp.dot(h, wd_ref[0], preferred_element_type=F32).astype(BF16)
        each_chunk(scatter, i, "start")
        each_chunk(gather, i + GATHER_AHEAD, "start")

        @pl.when(i > 0)
        def _():
            each_chunk(scatter, i - 1, "wait")

        @pl.when(i == nu - 1)
        def _():
            for ahead in range(1, GATHER_AHEAD + 1):
                each_chunk(gather, i + ahead, "wait")
            each_chunk(scatter, i, "wait")


def _experts(tile_expert, n_used, chunk_src, chunk_dst, xs, ys_rows, wg, wu, wd):
    d = xs.shape[1]
    f = wg.shape[2]
    tm = TILE_CHUNKS * CHUNK
    n_tiles = tile_expert.shape[0]
    wspec = lambda shape: pl.BlockSpec(shape, lambda i, te, nu, src, dst: (te[i], 0, 0))
    grid_spec = pltpu.PrefetchScalarGridSpec(
        num_scalar_prefetch=4,
        grid=(n_tiles,),
        in_specs=[pl.BlockSpec(memory_space=pl.ANY),
                  wspec((1, d, f)), wspec((1, d, f)), wspec((1, f, d)),
                  pl.BlockSpec(memory_space=pl.ANY)],
        out_specs=pl.BlockSpec(memory_space=pl.ANY),
        scratch_shapes=[pltpu.VMEM((X_SLOTS, tm, d), BF16), pltpu.VMEM((Y_SLOTS, tm, d), BF16),
                        pltpu.SemaphoreType.DMA((X_SLOTS,)), pltpu.SemaphoreType.DMA((Y_SLOTS,))],
    )
    return pl.pallas_call(
        _experts_kernel,
        out_shape=jax.ShapeDtypeStruct((ys_rows, d), BF16),
        grid_spec=grid_spec,
        input_output_aliases={8: 0},
        compiler_params=pltpu.CompilerParams(
            dimension_semantics=("arbitrary",), vmem_limit_bytes=VMEM_LIMIT),
        name="experts",
    )(tile_expert, n_used, chunk_src, chunk_dst, xs, wg, wu, wd, jnp.zeros((ys_rows, d), BF16))


def _combine_kernel(x1_ref, tok_ref, g_ref, ys_ref, o_ref):
    tm = x1_ref.shape[0]
    rows = ys_ref.shape[0]
    tok = tok_ref[...]
    pos1 = tok[:, 0:1].astype(jnp.int32)
    pos2 = tok[:, 1:2].astype(jnp.int32)
    ci = lax.broadcasted_iota(jnp.int32, (tm, rows), 1)
    unperm = jnp.where(ci == pos1, tok[:, 2:3], jnp.where(ci == pos2, tok[:, 3:4], 0.0))
    y = jnp.dot(unperm.astype(BF16), ys_ref[...], preferred_element_type=F32)
    o_ref[...] = _rms(x1_ref[...] + y, g_ref[...])


def _combine(x1, tok, g, ys, tm):
    n, d = x1.shape
    rows = _block_rows(tm)
    return pl.pallas_call(
        _combine_kernel,
        out_shape=jax.ShapeDtypeStruct((n, d), F32),
        grid=(n // tm,),
        in_specs=[pl.BlockSpec((tm, d), lambda i: (i, 0)),
                  pl.BlockSpec((tm, LANES), lambda i: (i, 0)),
                  pl.BlockSpec((1, d), lambda i: (0, 0)),
                  pl.BlockSpec((rows, d), lambda i: (i, 0))],
        out_specs=pl.BlockSpec((tm, d), lambda i: (i, 0)),
        compiler_params=pltpu.CompilerParams(
            dimension_semantics=("arbitrary",), vmem_limit_bytes=VMEM_LIMIT),
        name="combine",
    )(x1, tok, g, ys)


def _block_diag(w):
    nb, bd, _ = w.shape
    eye = jnp.eye(nb, dtype=w.dtype)
    return (eye[:, None, :, None] * w[:, :, None, :]).reshape(nb * bd, nb * bd)


def _tile_tables(nch, rows):
    nblk = nch.shape[0]
    ex_cum = lambda a, axis: jnp.cumsum(a, axis=axis) - a
    ch_off = ex_cum(nch, 1)
    blk_cum = ex_cum(nch, 0)
    tot = jnp.sum(nch, axis=0)
    tiles_e = (tot + TILE_CHUNKS - 1) // TILE_CHUNKS
    t_end = jnp.cumsum(tiles_e)
    t_start = t_end - tiles_e
    max_chunks = nblk * ((rows - N_EXPERTS * CHUNK) // CHUNK + N_EXPERTS)
    n_tiles = pl.cdiv(max_chunks, TILE_CHUNKS) + N_EXPERTS
    tid = jnp.arange(n_tiles + GATHER_AHEAD, dtype=jnp.int32)
    slot = jnp.arange(TILE_CHUNKS, dtype=jnp.int32)[None, :]
    te = jnp.minimum(jnp.sum((tid[:, None] >= t_end[None, :]).astype(jnp.int32), axis=1),
                     N_EXPERTS - 1)
    q = (tid - t_start[te])[:, None] * TILE_CHUNKS + slot
    valid = (tid < t_end[-1])[:, None] & (q < tot[te][:, None])
    seg_start = blk_cum.T[te]
    seg_end = seg_start + nch.T[te]
    bidx = jnp.minimum(jnp.sum((q[:, :, None] >= seg_end[:, None, :]).astype(jnp.int32), axis=-1),
                       nblk - 1)
    start_b = jnp.take_along_axis(seg_start, bidx, axis=1)
    off_b = jnp.take_along_axis(ch_off.T[te], bidx, axis=1)
    row = bidx * rows + CHUNK * (off_b + q - start_b)
    src = jnp.where(valid, row, rows - CHUNK).astype(jnp.int32)
    assert Y_SLOTS * TILE_CHUNKS * CHUNK <= rows
    spare = nblk * rows + CHUNK * ((tid % Y_SLOTS)[:, None] * TILE_CHUNKS + slot)
    dst = jnp.where(valid, row, spare).astype(jnp.int32)
    return (te[:n_tiles], t_end[-1:].astype(jnp.int32), src.reshape(-1),
            dst[:n_tiles].reshape(-1))


def kernel(x, mix_norm, w_in, b_forget, conv_w, conv_b, w_a, b_a, w_x, b_x, lru_lambda, w_out,
           ffn_norm, w_group, b_group, w_inner, b_inner, w_gate, w_up, w_down, final_norm):
    b, s, d = x.shape
    assert w_in.shape[0] == 1, "the final rmsnorm is fused into the single layer's combine stage"
    n = b * s
    tm = min(512, s)
    x2 = x.reshape(n, d)

    sizes = (ATT_WIDTH, ATT_WIDTH, ATT_WIDTH, ATT_HEADS, LRU_WIDTH, LRU_WIDTH)
    o = [0]
    for v in sizes:
        o.append(o[-1] + v)
    w_f = jnp.pad(w_in[0][:, o[3]:o[4]], ((0, 0), (0, LANES - ATT_HEADS)))
    w_cat = jnp.concatenate([w_in[0][:, :o[3]], w_in[0][:, o[4]:], w_f], axis=1).astype(BF16)
    qkv, gr, f = _in_proj(x2, mix_norm.reshape(1, d), w_cat, tm)

    bfg = jnp.pad(b_forget[0], (0, LANES - ATT_HEADS)).reshape(1, LANES)
    lru, cs = _lru(gr.reshape(b, s, -1), f.reshape(b, s, LANES),
                   conv_w[0], conv_b.reshape(1, -1),
                   _block_diag(w_a[0]).astype(BF16), b_a.reshape(1, -1),
                   _block_diag(w_x[0]).astype(BF16), b_x.reshape(1, -1),
                   lru_lambda.reshape(1, -1), bfg, tm)

    att = _attn(qkv.reshape(b, s, -1), cs, tm)

    wr = jnp.concatenate(
        [w_group[0], jnp.transpose(w_inner[0], (1, 0, 2)).reshape(d, N_EXPERTS)], axis=1)
    wr = jnp.pad(wr, ((0, 0), (0, LANES - wr.shape[1])))
    wr_hi = wr.astype(BF16)
    wr = jnp.stack([wr_hi, (wr - wr_hi.astype(F32)).astype(BF16)])
    br = jnp.concatenate([b_group[0], b_inner[0].reshape(-1)])
    br = jnp.pad(br, (0, LANES - br.shape[0])).reshape(1, LANES)
    x1, xs, tok, nch = _out_route(x2, att.reshape(n, -1), lru.reshape(n, -1),
                                  w_out[0].astype(BF16), ffn_norm.reshape(1, d), wr, br, tm)

    rows = _block_rows(tm)
    tables = _tile_tables(nch[:, 0, :N_EXPERTS].astype(jnp.int32), rows)
    fe = w_gate.shape[-1]
    ys = _experts(*tables, xs, xs.shape[0] + rows,
                  w_gate.reshape(N_EXPERTS, d, fe).astype(BF16),
                  w_up.reshape(N_EXPERTS, d, fe).astype(BF16),
                  w_down.reshape(N_EXPERTS, fe, d).astype(BF16))
    out = _combine(x1, tok, final_norm.reshape(1, d), ys, tm)
    return out.reshape(b, s, d)
```

```python
import functools

import jax
import jax.numpy as jnp
from jax import lax
from jax.experimental import pallas as pl
from jax.experimental.pallas import tpu as pltpu

ATT_HEADS = 8
HEAD_DIM = 64
ATT_WIDTH = ATT_HEADS * HEAD_DIM
LRU_WIDTH = 512
LRU_BLOCKS = 8
CONV_WIDTH = 4
LRU_C = 8.0
N_GROUPS = 4
EXPERTS_PER_GROUP = 8
N_EXPERTS = N_GROUPS * EXPERTS_PER_GROUP
D_EXPERT = 256
RMS_EPS = 1e-6
LANES = 128
SUBLANES = 8
CHUNK = 16
TILE_CHUNKS = 16
MASK_VALUE = -1e30
LOG2E = 1.4426950408889634
DECAY_PARTS = 3
VMEM_LIMIT = 48 * 1024 * 1024

BF16 = jnp.bfloat16
F32 = jnp.float32


def _rms(x, g):
    inv = lax.rsqrt(jnp.mean(x * x, axis=-1, keepdims=True) + RMS_EPS)
    return x * inv * g


def _block_rows(tm):
    return 2 * tm + N_EXPERTS * CHUNK


def _in_proj_kernel(x_ref, g_ref, w_ref, qkv_ref, gr_ref, f_ref):
    h = _rms(x_ref[...], g_ref[...]).astype(BF16)
    nq = qkv_ref.shape[1]
    ng = gr_ref.shape[1]
    aw = ATT_WIDTH
    q = jnp.dot(h, w_ref[:, :aw], preferred_element_type=F32) * (HEAD_DIM ** -0.5 * LOG2E)
    qkv_ref[:, :aw] = q.astype(BF16)
    qkv_ref[:, aw:] = jnp.dot(h, w_ref[:, aw:nq], preferred_element_type=F32).astype(BF16)
    gr_ref[...] = jnp.dot(h, w_ref[:, nq:nq + ng], preferred_element_type=F32)
    f_ref[...] = jnp.dot(h, w_ref[:, nq + ng:], preferred_element_type=F32)


def _in_proj(x2, g, w, tm):
    n, d = x2.shape
    nq, ng = 3 * ATT_WIDTH, 2 * LRU_WIDTH
    return pl.pallas_call(
        _in_proj_kernel,
        out_shape=(jax.ShapeDtypeStruct((n, nq), BF16),
                   jax.ShapeDtypeStruct((n, ng), F32),
                   jax.ShapeDtypeStruct((n, LANES), F32)),
        grid=(n // tm,),
        in_specs=[pl.BlockSpec((tm, d), lambda i: (i, 0)),
                  pl.BlockSpec((1, d), lambda i: (0, 0)),
                  pl.BlockSpec(w.shape, lambda i: (0, 0))],
        out_specs=(pl.BlockSpec((tm, nq), lambda i: (i, 0)),
                   pl.BlockSpec((tm, ng), lambda i: (i, 0)),
                   pl.BlockSpec((tm, LANES), lambda i: (i, 0))),
        compiler_params=pltpu.CompilerParams(
            dimension_semantics=("arbitrary",), vmem_limit_bytes=VMEM_LIMIT),
        name="in_proj",
    )(x2, g, w)


def _sigmoid(v):
    return 0.5 * jnp.tanh(0.5 * v) + 0.5


def _scan_rows(a, u, carry):
    ts, c = u.shape
    g = ts // SUBLANES
    u3 = u.reshape(g, SUBLANES, c)
    a3 = None if a is None else a.reshape(g, SUBLANES, c)
    sub = lax.broadcasted_iota(jnp.int32, u3.shape, 1)
    d = 1
    while d < SUBLANES:
        keep = sub >= d
        prev_u = jnp.where(keep, pltpu.roll(u3, d, 1), 0.0)
        if a3 is None:
            u3 = u3 + prev_u
        else:
            u3 = a3 * prev_u + u3
            a3 = a3 * jnp.where(keep, pltpu.roll(a3, d, 1), 1.0)
        d *= 2
    h_in = jnp.broadcast_to(carry, (SUBLANES, c))
    out = []
    for k in range(g):
        hk = u3[k] + (h_in if a3 is None else a3[k] * h_in)
        out.append(hk)
        h_in = jnp.broadcast_to(hk[SUBLANES - 1:, :], (SUBLANES, c))
    return jnp.concatenate(out, axis=0)


def _lru_kernel(gr_ref, f_ref, cw_ref, cb_ref, wa_ref, ba_ref, wx_ref, bx_ref, lam_ref, bf_ref,
                lru_ref, cs_ref, xprev, hprev, cprev):
    @pl.when(pl.program_id(1) == 0)
    def _():
        xprev[...] = jnp.zeros_like(xprev)
        hprev[...] = jnp.zeros_like(hprev)
        cprev[...] = jnp.zeros_like(cprev)

    ts = gr_ref.shape[1]
    w = LRU_WIDTH
    gate = gr_ref[0, :, :w]
    rec = gr_ref[0, :, w:]
    row8 = lax.broadcasted_iota(jnp.int32, (SUBLANES, w), 0)

    prev = xprev[...]
    conv = cb_ref[...] + cw_ref[CONV_WIDTH - 1:CONV_WIDTH, :] * rec
    for k in range(1, CONV_WIDTH):
        rolled = pltpu.roll(rec, k, 0)
        head = jnp.where(row8 < k, pltpu.roll(prev, k, 0), rolled[:SUBLANES])
        shifted = jnp.concatenate([head, rolled[SUBLANES:]], axis=0)
        conv = conv + cw_ref[CONV_WIDTH - 1 - k:CONV_WIDTH - k, :] * shifted
    xprev[...] = rec[ts - SUBLANES:, :]

    cb16 = conv.astype(BF16)
    r = _sigmoid(jnp.dot(cb16, wa_ref[...], preferred_element_type=F32) + ba_ref[...])
    i = _sigmoid(jnp.dot(cb16, wx_ref[...], preferred_element_type=F32) + bx_ref[...])
    log_a = (-LRU_C) * r * jax.nn.softplus(-lam_ref[...])
    a = jnp.exp(log_a)
    th = jnp.tanh(log_a)
    one_minus_a2 = -2.0 * th / (1.0 - th)
    mult = jnp.where(one_minus_a2 > 0.0, one_minus_a2 * lax.rsqrt(one_minus_a2), 0.0)
    u = mult * (i * conv)

    h = _scan_rows(a, u, hprev[...])
    hprev[...] = h[ts - 1:, :]
    lru_ref[0] = (h * jax.nn.gelu(gate, approximate=True)).astype(BF16)

    c = _scan_rows(None, jax.nn.log_sigmoid(f_ref[0] + bf_ref[...]), cprev[...])
    lanec = lax.broadcasted_iota(jnp.int32, (ts, LANES), 1)
    cprev[...] = c[ts - 1:, :]
    rest = c * (-LOG2E)
    parts = jnp.zeros_like(rest)
    for j in range(DECAY_PARTS):
        piece = rest.astype(BF16).astype(F32)
        rest = rest - piece
        shifted = piece if j == 0 else pltpu.roll(piece, ATT_HEADS * j, 1)
        parts = jnp.where((lanec >= ATT_HEADS * j) & (lanec < ATT_HEADS * (j + 1)), shifted, parts)
    cs_ref[0] = parts.astype(BF16)


def _lru(gr, f, cw, cb, wa, ba, wx, bx, lam, bfg, ts):
    b, s, _ = gr.shape
    w = LRU_WIDTH
    full = lambda shape: pl.BlockSpec(shape, lambda bi, ti: (0,) * len(shape))
    return pl.pallas_call(
        _lru_kernel,
        out_shape=(jax.ShapeDtypeStruct((b, s, w), BF16),
                   jax.ShapeDtypeStruct((b, s, LANES), BF16)),
        grid=(b, s // ts),
        in_specs=[pl.BlockSpec((1, ts, 2 * w), lambda bi, ti: (bi, ti, 0)),
                  pl.BlockSpec((1, ts, LANES), lambda bi, ti: (bi, ti, 0)),
                  full((CONV_WIDTH, w)), full((1, w)),
                  full((w, w)), full((1, w)), full((w, w)), full((1, w)),
                  full((1, w)), full((1, LANES))],
        out_specs=(pl.BlockSpec((1, ts, w), lambda bi, ti: (bi, ti, 0)),
                   pl.BlockSpec((1, ts, LANES), lambda bi, ti: (bi, ti, 0))),
        scratch_shapes=[pltpu.VMEM((SUBLANES, w), F32),
                        pltpu.VMEM((1, w), F32),
                        pltpu.VMEM((1, LANES), F32)],
        compiler_params=pltpu.CompilerParams(
            dimension_semantics=("arbitrary", "arbitrary"), vmem_limit_bytes=VMEM_LIMIT),
        name="lru",
    )(gr, f, cw, cb, wa, ba, wx, bx, lam, bfg)


def _attn_kernel(q_ref, k_ref, v_ref, cs_ref, o_ref, ka_e, ka_o, va_e, va_o,
                 m_e, m_o, acc_e, acc_o, *, t):
    head0 = 2 * pl.program_id(1)
    i = pl.program_id(2)
    lane = lax.broadcasted_iota(jnp.int32, (1, LANES), 1)
    lo = lane < HEAD_DIM
    den_e = HEAD_DIM

    def ones_where(cond, rows):
        return jnp.broadcast_to(jnp.where(cond, 1.0, 0.0), (rows, LANES)).astype(BF16)

    @pl.when(i == 0)
    def _():
        r = lax.broadcasted_iota(jnp.int32, (LANES, LANES), 0)
        c = lax.broadcasted_iota(jnp.int32, (LANES, LANES), 1)

        def pieces(head, base):
            j = c - base
            sel = jnp.where((j >= 0) & (j < DECAY_PARTS) & (r == head + ATT_HEADS * j), 1.0, 0.0)
            return jnp.dot(cs_ref[0], sel.astype(BF16), preferred_element_type=F32).astype(BF16)

        k2 = k_ref[0]
        v2 = v_ref[0]
        ka_e[...] = jnp.where(lo, k2, pieces(head0, HEAD_DIM))
        ka_o[...] = jnp.where(lo, pieces(head0 + 1, 0), k2)
        va_e[...] = jnp.where(lo, v2, ones_where(lane == den_e, v2.shape[0]))
        va_o[...] = jnp.where(lo, ones_where(lane == 0, v2.shape[0]), v2)

    q = q_ref[0]
    qa_e = jnp.where(lo, q, ones_where(lane < HEAD_DIM + DECAY_PARTS, t))
    qa_o = jnp.where(lo, ones_where(lane < DECAY_PARTS, t), q)
    nt = (((1,), (1,)), ((), ()))

    heads = ((qa_e, ka_e, va_e, m_e, acc_e), (qa_o, ka_o, va_o, m_o, acc_o))
    for _, _, _, m_ref, acc_ref in heads:
        m_ref[...] = jnp.full(m_ref.shape, MASK_VALUE, F32)
        acc_ref[...] = jnp.zeros(acc_ref.shape, F32)

    def block(start, width, first_masked_col):
        if not isinstance(start, int):
            start = pl.multiple_of(start, t)
        if first_masked_col is not None:
            rr = lax.broadcasted_iota(jnp.int32, (t, width), 0)
            cc = lax.broadcasted_iota(jnp.int32, (t, width), 1)
            keep = cc - first_masked_col <= rr
        updates = []
        for qa, ka, va, m_ref, acc_ref in heads:
            s = lax.dot_general(qa, ka[pl.ds(start, width), :], nt, preferred_element_type=F32)
            if first_masked_col is not None:
                s = jnp.where(keep, s, MASK_VALUE)
            cols = [s[:, c * LANES:(c + 1) * LANES] for c in range(width // LANES)]
            m_old = m_ref[...]
            row_max = jnp.max(functools.reduce(jnp.maximum, cols), axis=-1, keepdims=True)
            m_new = jnp.maximum(m_old, row_max)
            p = jnp.concatenate([jnp.exp2(c - m_new) for c in cols], axis=1).astype(BF16)
            acc = acc_ref[...] * jnp.exp2(m_old - m_new) + jnp.dot(
                p, va[pl.ds(start, width), :], preferred_element_type=F32)
            updates.append((m_ref, m_new, acc_ref, acc))
        for m_ref, m_new, acc_ref, acc in updates:
            m_ref[...] = m_new
            acc_ref[...] = acc

    before = jnp.maximum(i - 1, 0)

    def double(jj, c):
        block(jj * (2 * t), 2 * t, None)
        return c

    lax.fori_loop(0, before // 2, double, 0)

    @pl.when(before % 2 == 1)
    def _():
        block((before - 1) * t, t, None)

    @pl.when(i > 0)
    def _():
        block((i - 1) * t, 2 * t, t)

    @pl.when(i == 0)
    def _():
        block(0, t, 0)

    a_e = acc_e[...]
    a_o = acc_o[...]
    o_ref[0] = jnp.where(lo, a_e / a_e[:, den_e:den_e + 1], a_o / a_o[:, 0:1]).astype(BF16)


def _attn(qkv, cs, t):
    b, s, _ = qkv.shape
    hp = ATT_WIDTH // LANES
    return pl.pallas_call(
        functools.partial(_attn_kernel, t=t),
        out_shape=jax.ShapeDtypeStruct((b, s, ATT_WIDTH), BF16),
        grid=(b, hp, s // t),
        in_specs=[pl.BlockSpec((1, t, LANES), lambda bi, h, i: (bi, i, h)),
                  pl.BlockSpec((1, s, LANES), lambda bi, h, i: (bi, 0, hp + h)),
                  pl.BlockSpec((1, s, LANES), lambda bi, h, i: (bi, 0, 2 * hp + h)),
                  pl.BlockSpec((1, s, LANES), lambda bi, h, i: (bi, 0, 0))],
        out_specs=pl.BlockSpec((1, t, LANES), lambda bi, h, i: (bi, i, h)),
        scratch_shapes=([pltpu.VMEM((s, LANES), BF16)] * 4 + [pltpu.VMEM((t, LANES), F32)] * 2
                        + [pltpu.VMEM((t, LANES), F32)] * 2),
        compiler_params=pltpu.CompilerParams(
            dimension_semantics=("arbitrary", "arbitrary", "arbitrary"),
            vmem_limit_bytes=VMEM_LIMIT),
        name="attn",
    )(qkv, qkv, qkv, cs)


def _out_route_kernel(*refs):
    xs_ref = refs[8]
    last = pl.num_programs(0) - 1
    pl.when(pl.program_id(0) < last)(functools.partial(_route_block, *refs))

    @pl.when(pl.program_id(0) == last)
    def _():
        xs_ref[...] = jnp.zeros_like(xs_ref)


def _route_block(x_ref, att_ref, lru_ref, wo_ref, g_ref, wr_ref, br_ref,
                 x1_ref, xs_ref, tok_ref, nch_ref):
    tm = x_ref.shape[0]
    aw = att_ref.shape[1]
    x1 = (x_ref[...]
          + jnp.dot(att_ref[...], wo_ref[:aw, :], preferred_element_type=F32)
          + jnp.dot(lru_ref[...], wo_ref[aw:, :], preferred_element_type=F32))
    x1_ref[...] = x1
    h2 = _rms(x1, g_ref[...])
    h2_hi = h2.astype(BF16)
    h2_lo = (h2 - h2_hi.astype(F32)).astype(BF16)

    logits = (jnp.dot(h2_hi, wr_ref[0], preferred_element_type=F32)
              + jnp.dot(h2_lo, wr_ref[0], preferred_element_type=F32)
              + jnp.dot(h2_hi, wr_ref[1], preferred_element_type=F32)) + br_ref[...]
    lane = lax.broadcasted_iota(jnp.int32, (tm, LANES), 1)
    gl = jnp.where(lane < N_GROUPS, logits, MASK_VALUE)
    gmax = jnp.max(gl, axis=-1, keepdims=True)
    gsum = jnp.sum(jnp.exp(gl - gmax), axis=-1, keepdims=True)
    g_w = 1.0 / gsum
    g_idx = jnp.min(jnp.where(gl == gmax, lane, LANES), axis=-1, keepdims=True)
    in_group = (lane >= N_GROUPS) & (lane < N_GROUPS + N_EXPERTS) & (
        lax.shift_right_logical(lane - N_GROUPS, 3) == g_idx)
    il = jnp.where(in_group, logits, MASK_VALUE)
    m1 = jnp.max(il, axis=-1, keepdims=True)
    e1 = jnp.min(jnp.where(il == m1, lane, LANES), axis=-1, keepdims=True)
    il2 = jnp.where(lane == e1, MASK_VALUE, il)
    m2 = jnp.max(il2, axis=-1, keepdims=True)
    e2 = jnp.min(jnp.where(il2 == m2, lane, LANES), axis=-1, keepdims=True)
    ratio = jnp.exp(m2 - m1)
    w1 = g_w / (1.0 + ratio)
    w2 = w1 * ratio
    hot1 = lane == e1 - N_GROUPS
    hot2 = lane == e2 - N_GROUPS

    onehot = jnp.where(hot1 | hot2, 1.0, 0.0)
    rr = lax.broadcasted_iota(jnp.int32, (tm, tm), 0)
    cc = lax.broadcasted_iota(jnp.int32, (tm, tm), 1)
    tri = jnp.where(cc < rr, 1.0, 0.0).astype(BF16)
    before = jnp.dot(tri, onehot.astype(BF16), preferred_element_type=F32)
    cnt = jnp.sum(onehot, axis=0, keepdims=True)
    nch = jnp.floor((cnt + (CHUNK - 1)) * (1.0 / CHUNK))
    r128 = lax.broadcasted_iota(jnp.int32, (LANES, LANES), 0)
    c128 = lax.broadcasted_iota(jnp.int32, (LANES, LANES), 1)
    upper = jnp.where(r128 < c128, 1.0, 0.0).astype(BF16)
    nch8 = jnp.broadcast_to(nch, (SUBLANES, LANES))
    seg_off = CHUNK * jnp.dot(nch8.astype(BF16), upper, preferred_element_type=F32)[0:1, :]
    pos = seg_off + before
    pos1 = jnp.sum(jnp.where(hot1, pos, 0.0), axis=-1, keepdims=True)
    pos2 = jnp.sum(jnp.where(hot2, pos, 0.0), axis=-1, keepdims=True)
    nch_ref[0] = nch8

    tok = jnp.where(lane == 0, pos1, 0.0)
    tok = jnp.where(lane == 1, pos2, tok)
    tok = jnp.where(lane == 2, w1, tok)
    tok = jnp.where(lane == 3, w2, tok)
    tok_ref[...] = tok

    tok_t = jnp.transpose(tok)
    p1 = tok_t[0:1, :].astype(jnp.int32)
    p2 = tok_t[1:2, :].astype(jnp.int32)
    rows = xs_ref.shape[0]
    ri = lax.broadcasted_iota(jnp.int32, (rows, tm), 0)
    perm = jnp.where((ri == p1) | (ri == p2), 1.0, 0.0).astype(BF16)
    xs_ref[...] = jnp.dot(perm, h2_hi, preferred_element_type=F32).astype(BF16)


def _out_route(x2, att, lru, wo, g, wr, br, tm):
    n, d = x2.shape
    aw = att.shape[1]
    nblk = n // tm
    rows = _block_rows(tm)
    blk = lambda i: jnp.minimum(i, nblk - 1)
    row = lambda c: pl.BlockSpec((tm, c), lambda i: (blk(i), 0))
    full = lambda shape: pl.BlockSpec(shape, lambda i: (0,) * len(shape))
    return pl.pallas_call(
        _out_route_kernel,
        out_shape=(jax.ShapeDtypeStruct((n, d), F32),
                   jax.ShapeDtypeStruct(((nblk + 1) * rows, d), BF16),
                   jax.ShapeDtypeStruct((n, LANES), F32),
                   jax.ShapeDtypeStruct((nblk, SUBLANES, LANES), F32)),
        grid=(nblk + 1,),
        in_specs=[row(d), row(aw), row(d - aw), full((d, d)), full((1, d)),
                  full((2, d, LANES)), full((1, LANES))],
        out_specs=(row(d), pl.BlockSpec((rows, d), lambda i: (i, 0)), row(LANES),
                   pl.BlockSpec((1, SUBLANES, LANES), lambda i: (blk(i), 0, 0))),
        compiler_params=pltpu.CompilerParams(
            dimension_semantics=("arbitrary",), vmem_limit_bytes=VMEM_LIMIT),
        name="out_route",
    )(x2, att, lru, wo, g, wr, br)


GATHER_AHEAD = 2
X_SLOTS = GATHER_AHEAD + 1
Y_SLOTS = 2


def _experts_kernel(te_ref, nu_ref, src_ref, dst_ref, xs_hbm, wg_ref, wu_ref, wd_ref,
                    ys_hbm, xbuf, ybuf, wg16, wu16, wd16, gsem, ssem):
    del xs_hbm
    i = pl.program_id(0)
    nu = nu_ref[0]

    def gather(tile, k):
        row = pl.multiple_of(src_ref[tile * TILE_CHUNKS + k], CHUNK)
        slot = tile % X_SLOTS
        return pltpu.make_async_copy(ys_hbm.at[pl.ds(row, CHUNK), :],
                                     xbuf.at[slot, pl.ds(k * CHUNK, CHUNK), :], gsem.at[slot])

    def scatter(tile, k):
        row = pl.multiple_of(dst_ref[tile * TILE_CHUNKS + k], CHUNK)
        slot = tile % Y_SLOTS
        return pltpu.make_async_copy(ybuf.at[slot, pl.ds(k * CHUNK, CHUNK), :],
                                     ys_hbm.at[pl.ds(row, CHUNK), :], ssem.at[slot])

    def each_chunk(copy, tile, op):
        for k in range(TILE_CHUNKS):
            getattr(copy(tile, k), op)()

    @pl.when((i == 0) & (nu > 0))
    def _():
        for ahead in range(GATHER_AHEAD):
            each_chunk(gather, ahead, "start")

    @pl.when((i < nu) & ((i == 0) | (te_ref[i] != te_ref[jnp.maximum(i - 1, 0)])))
    def _():
        wg16[...] = wg_ref[0].astype(BF16)
        wu16[...] = wu_ref[0].astype(BF16)
        wd16[...] = wd_ref[0].astype(BF16)

    @pl.when(i < nu)
    def _():
        each_chunk(gather, i, "wait")
        x = xbuf[i % X_SLOTS]
        g = jnp.dot(x, wg16[...], preferred_element_type=F32)
        u = jnp.dot(x, wu16[...], preferred_element_type=F32)
        h = (g * _sigmoid(g) * u).astype(BF16)
        ybuf[i % Y_SLOTS] = jnp.dot(h, wd16[...], preferred_element_type=F32).astype(BF16)
        each_chunk(scatter, i, "start")
        each_chunk(gather, i + GATHER_AHEAD, "start")

        @pl.when(i > 0)
        def _():
            each_chunk(scatter, i - 1, "wait")

        @pl.when(i == nu - 1)
        def _():
            for ahead in range(1, GATHER_AHEAD + 1):
                each_chunk(gather, i + ahead, "wait")
            each_chunk(scatter, i, "wait")


def _experts(tile_expert, n_used, chunk_src, chunk_dst, xs, wg, wu, wd):
    d = xs.shape[1]
    f = wg.shape[2]
    tm = TILE_CHUNKS * CHUNK
    n_tiles = tile_expert.shape[0]
    wspec = lambda shape: pl.BlockSpec(shape, lambda i, te, nu, src, dst: (te[i], 0, 0))
    grid_spec = pltpu.PrefetchScalarGridSpec(
        num_scalar_prefetch=4,
        grid=(n_tiles,),
        in_specs=[pl.BlockSpec(memory_space=pl.ANY),
                  wspec((1, d, f)), wspec((1, d, f)), wspec((1, f, d))],
        out_specs=pl.BlockSpec(memory_space=pl.ANY),
        scratch_shapes=[pltpu.VMEM((X_SLOTS, tm, d), BF16), pltpu.VMEM((Y_SLOTS, tm, d), BF16),
                        pltpu.VMEM((d, f), BF16), pltpu.VMEM((d, f), BF16), pltpu.VMEM((f, d), BF16),
                        pltpu.SemaphoreType.DMA((X_SLOTS,)), pltpu.SemaphoreType.DMA((Y_SLOTS,))],
    )
    return pl.pallas_call(
        _experts_kernel,
        out_shape=jax.ShapeDtypeStruct(xs.shape, BF16),
        grid_spec=grid_spec,
        input_output_aliases={4: 0},
        compiler_params=pltpu.CompilerParams(
            dimension_semantics=("arbitrary",), vmem_limit_bytes=VMEM_LIMIT),
        name="experts",
    )(tile_expert, n_used, chunk_src, chunk_dst, xs, wg, wu, wd)


def _combine_kernel(x1_ref, tok_ref, g_ref, ys_ref, o_ref):
    tm = x1_ref.shape[0]
    rows = ys_ref.shape[0]
    tok = tok_ref[...]
    pos1 = tok[:, 0:1].astype(jnp.int32)
    pos2 = tok[:, 1:2].astype(jnp.int32)
    ci = lax.broadcasted_iota(jnp.int32, (tm, rows), 1)
    unperm = jnp.where(ci == pos1, tok[:, 2:3], jnp.where(ci == pos2, tok[:, 3:4], 0.0))
    y = jnp.dot(unperm.astype(BF16), ys_ref[...], preferred_element_type=F32)
    o_ref[...] = _rms(x1_ref[...] + y, g_ref[...])


def _combine(x1, tok, g, ys, tm):
    n, d = x1.shape
    rows = _block_rows(tm)
    return pl.pallas_call(
        _combine_kernel,
        out_shape=jax.ShapeDtypeStruct((n, d), F32),
        grid=(n // tm,),
        in_specs=[pl.BlockSpec((tm, d), lambda i: (i, 0)),
                  pl.BlockSpec((tm, LANES), lambda i: (i, 0)),
                  pl.BlockSpec((1, d), lambda i: (0, 0)),
                  pl.BlockSpec((rows, d), lambda i: (i, 0))],
        out_specs=pl.BlockSpec((tm, d), lambda i: (i, 0)),
        compiler_params=pltpu.CompilerParams(
            dimension_semantics=("arbitrary",), vmem_limit_bytes=VMEM_LIMIT),
        name="combine",
    )(x1, tok, g, ys)


def _block_diag(w):
    nb, bd, _ = w.shape
    eye = jnp.eye(nb, dtype=w.dtype)
    return (eye[:, None, :, None] * w[:, :, None, :]).reshape(nb * bd, nb * bd)


def _tile_tables(nch, rows):
    nblk = nch.shape[0]
    ex_cum = lambda a, axis: jnp.cumsum(a, axis=axis) - a
    ch_off = ex_cum(nch, 1)
    blk_cum = ex_cum(nch, 0)
    tot = jnp.sum(nch, axis=0)
    tiles_e = (tot + TILE_CHUNKS - 1) // TILE_CHUNKS
    t_end = jnp.cumsum(tiles_e)
    t_start = t_end - tiles_e
    max_chunks = nblk * ((rows - N_EXPERTS * CHUNK) // CHUNK + N_EXPERTS)
    n_tiles = pl.cdiv(max_chunks, TILE_CHUNKS) + N_EXPERTS
    tid = jnp.arange(n_tiles + GATHER_AHEAD, dtype=jnp.int32)
    slot = jnp.arange(TILE_CHUNKS, dtype=jnp.int32)[None, :]
    te = jnp.minimum(jnp.sum((tid[:, None] >= t_end[None, :]).astype(jnp.int32), axis=1),
                     N_EXPERTS - 1)
    is_te = te[:, None] == jnp.arange(N_EXPERTS, dtype=jnp.int32)[None, :]
    of_te = lambda v: jnp.sum(jnp.where(is_te[(...,) + (None,) * (v.ndim - 1)], v[None], 0), axis=1)
    q = (tid - of_te(t_start))[:, None] * TILE_CHUNKS + slot
    valid = (tid < t_end[-1])[:, None] & (q < of_te(tot)[:, None])
    seg_start = of_te(blk_cum.T)
    seg_end = seg_start + of_te(nch.T)
    bidx = jnp.minimum(jnp.sum((q[:, :, None] >= seg_end[:, None, :]).astype(jnp.int32), axis=-1),
                       nblk - 1)
    is_b = bidx[:, :, None] == jnp.arange(nblk, dtype=jnp.int32)[None, None, :]
    of_b = lambda v: jnp.sum(jnp.where(is_b, v[:, None, :], 0), axis=-1)
    row = bidx * rows + CHUNK * (of_b(of_te(ch_off.T)) + q - of_b(seg_start))
    src = jnp.where(valid, row, rows - CHUNK).astype(jnp.int32)
    assert Y_SLOTS * TILE_CHUNKS * CHUNK <= rows
    spare = nblk * rows + CHUNK * ((tid % Y_SLOTS)[:, None] * TILE_CHUNKS + slot)
    dst = jnp.where(valid, row, spare).astype(jnp.int32)
    return (te[:n_tiles], t_end[-1:].astype(jnp.int32), src.reshape(-1),
            dst[:n_tiles].reshape(-1))


def kernel(x, mix_norm, w_in, b_forget, conv_w, conv_b, w_a, b_a, w_x, b_x, lru_lambda, w_out,
           ffn_norm, w_group, b_group, w_inner, b_inner, w_gate, w_up, w_down, final_norm):
    b, s, d = x.shape
    assert w_in.shape[0] == 1, "the final rmsnorm is fused into the single layer's combine stage"
    n = b * s
    tm = min(512, s)
    x2 = x.reshape(n, d)

    sizes = (ATT_WIDTH, ATT_WIDTH, ATT_WIDTH, ATT_HEADS, LRU_WIDTH, LRU_WIDTH)
    o = [0]
    for v in sizes:
        o.append(o[-1] + v)
    w_f = jnp.pad(w_in[0][:, o[3]:o[4]], ((0, 0), (0, LANES - ATT_HEADS)))
    w_cat = jnp.concatenate([w_in[0][:, :o[3]], w_in[0][:, o[4]:], w_f], axis=1).astype(BF16)
    qkv, gr, f = _in_proj(x2, mix_norm.reshape(1, d), w_cat, tm)

    bfg = jnp.pad(b_forget[0], (0, LANES - ATT_HEADS)).reshape(1, LANES)
    lru, cs = _lru(gr.reshape(b, s, -1), f.reshape(b, s, LANES),
                   conv_w[0], conv_b.reshape(1, -1),
                   _block_diag(w_a[0]).astype(BF16), b_a.reshape(1, -1),
                   _block_diag(w_x[0]).astype(BF16), b_x.reshape(1, -1),
                   lru_lambda.reshape(1, -1), bfg, tm)

    att = _attn(qkv.reshape(b, s, -1), cs, tm)

    wr = jnp.concatenate(
        [w_group[0], jnp.transpose(w_inner[0], (1, 0, 2)).reshape(d, N_EXPERTS)], axis=1)
    wr = jnp.pad(wr, ((0, 0), (0, LANES - wr.shape[1])))
    wr_hi = wr.astype(BF16)
    wr = jnp.stack([wr_hi, (wr - wr_hi.astype(F32)).astype(BF16)])
    br = jnp.concatenate([b_group[0], b_inner[0].reshape(-1)])
    br = jnp.pad(br, (0, LANES - br.shape[0])).reshape(1, LANES)
    x1, xs, tok, nch = _out_route(x2, att.reshape(n, -1), lru.reshape(n, -1),
                                  w_out[0].astype(BF16), ffn_norm.reshape(1, d), wr, br, tm)

    rows = _block_rows(tm)
    tables = _tile_tables(nch[:, 0, :N_EXPERTS].astype(jnp.int32), rows)
    fe = w_gate.shape[-1]
    ys = _experts(*tables, xs, w_gate.reshape(N_EXPERTS, d, fe), w_up.reshape(N_EXPERTS, d, fe),
                  w_down.reshape(N_EXPERTS, fe, d))
    out = _combine(x1, tok, final_norm.reshape(1, d), ys, tm)
    return out.reshape(b, s, d)
```

```python
import functools

import jax
import jax.numpy as jnp
from jax import lax
from jax.experimental import pallas as pl
from jax.experimental.pallas import tpu as pltpu

ATT_HEADS = 8
HEAD_DIM = 64
ATT_WIDTH = ATT_HEADS * HEAD_DIM
LRU_WIDTH = 512
LRU_BLOCKS = 8
CONV_WIDTH = 4
LRU_C = 8.0
N_GROUPS = 4
EXPERTS_PER_GROUP = 8
N_EXPERTS = N_GROUPS * EXPERTS_PER_GROUP
D_EXPERT = 256
RMS_EPS = 1e-6
LANES = 128
SUBLANES = 8
CHUNK = 16
TILE_CHUNKS = 16
MASK_VALUE = -1e30
LOG2E = 1.4426950408889634
DECAY_PARTS = 3
VMEM_LIMIT = 48 * 1024 * 1024

BF16 = jnp.bfloat16
F32 = jnp.float32


def _rms(x, g):
    inv = lax.rsqrt(jnp.mean(x * x, axis=-1, keepdims=True) + RMS_EPS)
    return x * inv * g


def _block_rows(tm):
    return 2 * tm + N_EXPERTS * CHUNK


def _sigmoid(v):
    return 0.5 * jnp.tanh(0.5 * v) + 0.5


def _scan_rows(a, u, carry):
    ts, c = u.shape
    g = ts // SUBLANES
    u3 = u.reshape(g, SUBLANES, c)
    a3 = None if a is None else a.reshape(g, SUBLANES, c)
    sub = lax.broadcasted_iota(jnp.int32, u3.shape, 1)
    d = 1
    while d < SUBLANES:
        keep = sub >= d
        prev_u = jnp.where(keep, pltpu.roll(u3, d, 1), 0.0)
        if a3 is None:
            u3 = u3 + prev_u
        else:
            u3 = a3 * prev_u + u3
            a3 = a3 * jnp.where(keep, pltpu.roll(a3, d, 1), 1.0)
        d *= 2
    h_in = jnp.broadcast_to(carry, (SUBLANES, c))
    out = []
    for k in range(g):
        hk = u3[k] + (h_in if a3 is None else a3[k] * h_in)
        out.append(hk)
        h_in = jnp.broadcast_to(hk[SUBLANES - 1:, :], (SUBLANES, c))
    return jnp.concatenate(out, axis=0)


def _mix_in_kernel(x_ref, g_ref, w_ref, cw_ref, cb_ref, wa_ref, ba_ref, wx_ref, bx_ref, lam_ref,
                   bf_ref, qkv_ref, lru_ref, cs_ref, xprev, hprev, cprev):
    @pl.when(pl.program_id(1) == 0)
    def _():
        xprev[...] = jnp.zeros_like(xprev)
        hprev[...] = jnp.zeros_like(hprev)
        cprev[...] = jnp.zeros_like(cprev)

    ts = x_ref.shape[1]
    w = LRU_WIDTH
    aw = ATT_WIDTH
    nq = 3 * aw
    hn = _rms(x_ref[0], g_ref[...]).astype(BF16)
    gate = jnp.dot(hn, w_ref[:, nq:nq + w], preferred_element_type=F32)
    rec = jnp.dot(hn, w_ref[:, nq + w:nq + 2 * w], preferred_element_type=F32)
    f_logit = jnp.dot(hn, w_ref[:, nq + 2 * w:], preferred_element_type=F32)
    q = jnp.dot(hn, w_ref[:, :aw], preferred_element_type=F32) * (HEAD_DIM ** -0.5 * LOG2E)
    qkv_ref[0, :, :aw] = q.astype(BF16)
    qkv_ref[0, :, aw:] = jnp.dot(hn, w_ref[:, aw:nq], preferred_element_type=F32).astype(BF16)
    row8 = lax.broadcasted_iota(jnp.int32, (SUBLANES, w), 0)

    prev = xprev[...]
    conv = cb_ref[...] + cw_ref[CONV_WIDTH - 1:CONV_WIDTH, :] * rec
    for k in range(1, CONV_WIDTH):
        rolled = pltpu.roll(rec, k, 0)
        head = jnp.where(row8 < k, pltpu.roll(prev, k, 0), rolled[:SUBLANES])
        shifted = jnp.concatenate([head, rolled[SUBLANES:]], axis=0)
        conv = conv + cw_ref[CONV_WIDTH - 1 - k:CONV_WIDTH - k, :] * shifted
    xprev[...] = rec[ts - SUBLANES:, :]

    cb16 = conv.astype(BF16)
    r = _sigmoid(jnp.dot(cb16, wa_ref[...], preferred_element_type=F32) + ba_ref[...])
    i = _sigmoid(jnp.dot(cb16, wx_ref[...], preferred_element_type=F32) + bx_ref[...])
    log_a = (-LRU_C) * r * jax.nn.softplus(-lam_ref[...])
    a = jnp.exp(log_a)
    th = jnp.tanh(log_a)
    one_minus_a2 = -2.0 * th / (1.0 - th)
    mult = jnp.where(one_minus_a2 > 0.0, one_minus_a2 * lax.rsqrt(one_minus_a2), 0.0)
    u = mult * (i * conv)

    h = _scan_rows(a, u, hprev[...])
    hprev[...] = h[ts - 1:, :]
    lru_ref[0] = (h * jax.nn.gelu(gate, approximate=True)).astype(BF16)

    c = _scan_rows(None, jax.nn.log_sigmoid(f_logit + bf_ref[...]), cprev[...])
    lanec = lax.broadcasted_iota(jnp.int32, (ts, LANES), 1)
    cprev[...] = c[ts - 1:, :]
    rest = c * (-LOG2E)
    parts = jnp.zeros_like(rest)
    for j in range(DECAY_PARTS):
        piece = rest.astype(BF16).astype(F32)
        rest = rest - piece
        shifted = piece if j == 0 else pltpu.roll(piece, ATT_HEADS * j, 1)
        parts = jnp.where((lanec >= ATT_HEADS * j) & (lanec < ATT_HEADS * (j + 1)), shifted, parts)
    cs_ref[0] = parts.astype(BF16)


def _mix_in(x, g, w_cat, cw, cb, wa, ba, wx, bx, lam, bfg, ts):
    b, s, d = x.shape
    w = LRU_WIDTH
    full = lambda shape: pl.BlockSpec(shape, lambda bi, ti: (0,) * len(shape))
    tile = lambda c: pl.BlockSpec((1, ts, c), lambda bi, ti: (bi, ti, 0))
    return pl.pallas_call(
        _mix_in_kernel,
        out_shape=(jax.ShapeDtypeStruct((b, s, 3 * ATT_WIDTH), BF16),
                   jax.ShapeDtypeStruct((b, s, w), BF16),
                   jax.ShapeDtypeStruct((b, s, LANES), BF16)),
        grid=(b, s // ts),
        in_specs=[tile(d), full((1, d)), full(w_cat.shape),
                  full((CONV_WIDTH, w)), full((1, w)),
                  full((w, w)), full((1, w)), full((w, w)), full((1, w)),
                  full((1, w)), full((1, LANES))],
        out_specs=(tile(3 * ATT_WIDTH), tile(w), tile(LANES)),
        scratch_shapes=[pltpu.VMEM((SUBLANES, w), F32),
                        pltpu.VMEM((1, w), F32),
                        pltpu.VMEM((1, LANES), F32)],
        compiler_params=pltpu.CompilerParams(
            dimension_semantics=("arbitrary", "arbitrary"), vmem_limit_bytes=VMEM_LIMIT),
        name="mix_in",
    )(x, g, w_cat, cw, cb, wa, ba, wx, bx, lam, bfg)


def _attn_kernel(q_ref, k_ref, v_ref, cs_ref, o_ref, ka_e, ka_o, va_e, va_o,
                 m_e, m_o, acc_e, acc_o, *, t):
    head0 = 2 * pl.program_id(1)
    i = pl.program_id(2)
    lane = lax.broadcasted_iota(jnp.int32, (1, LANES), 1)
    lo = lane < HEAD_DIM
    den_e = HEAD_DIM

    def ones_where(cond, rows):
        return jnp.broadcast_to(jnp.where(cond, 1.0, 0.0), (rows, LANES)).astype(BF16)

    @pl.when(i == 0)
    def _():
        r = lax.broadcasted_iota(jnp.int32, (LANES, LANES), 0)
        c = lax.broadcasted_iota(jnp.int32, (LANES, LANES), 1)

        def pieces(head, base):
            j = c - base
            sel = jnp.where((j >= 0) & (j < DECAY_PARTS) & (r == head + ATT_HEADS * j), 1.0, 0.0)
            return jnp.dot(cs_ref[0], sel.astype(BF16), preferred_element_type=F32).astype(BF16)

        k2 = k_ref[0]
        v2 = v_ref[0]
        ka_e[...] = jnp.where(lo, k2, pieces(head0, HEAD_DIM))
        ka_o[...] = jnp.where(lo, pieces(head0 + 1, 0), k2)
        va_e[...] = jnp.where(lo, v2, ones_where(lane == den_e, v2.shape[0]))
        va_o[...] = jnp.where(lo, ones_where(lane == 0, v2.shape[0]), v2)

    q = q_ref[0]
    qa_e = jnp.where(lo, q, ones_where(lane < HEAD_DIM + DECAY_PARTS, t))
    qa_o = jnp.where(lo, ones_where(lane < DECAY_PARTS, t), q)
    nt = (((1,), (1,)), ((), ()))

    heads = ((qa_e, ka_e, va_e, m_e, acc_e), (qa_o, ka_o, va_o, m_o, acc_o))
    for _, _, _, m_ref, acc_ref in heads:
        m_ref[...] = jnp.full(m_ref.shape, MASK_VALUE, F32)
        acc_ref[...] = jnp.zeros(acc_ref.shape, F32)

    def block(start, width, first_masked_col):
        if not isinstance(start, int):
            start = pl.multiple_of(start, t)
        if first_masked_col is not None:
            rr = lax.broadcasted_iota(jnp.int32, (t, width), 0)
            cc = lax.broadcasted_iota(jnp.int32, (t, width), 1)
            keep = cc - first_masked_col <= rr
        updates = []
        for qa, ka, va, m_ref, acc_ref in heads:
            s = lax.dot_general(qa, ka[pl.ds(start, width), :], nt, preferred_element_type=F32)
            if first_masked_col is not None:
                s = jnp.where(keep, s, MASK_VALUE)
            cols = [s[:, c * LANES:(c + 1) * LANES] for c in range(width // LANES)]
            m_old = m_ref[...]
            row_max = jnp.max(functools.reduce(jnp.maximum, cols), axis=-1, keepdims=True)
            m_new = jnp.maximum(m_old, row_max)
            p = jnp.concatenate([jnp.exp2(c - m_new) for c in cols], axis=1).astype(BF16)
            acc = acc_ref[...] * jnp.exp2(m_old - m_new) + jnp.dot(
                p, va[pl.ds(start, width), :], preferred_element_type=F32)
            updates.append((m_ref, m_new, acc_ref, acc))
        for m_ref, m_new, acc_ref, acc in updates:
            m_ref[...] = m_new
            acc_ref[...] = acc

    before = jnp.maximum(i - 1, 0)

    def double(jj, c):
        block(jj * (2 * t), 2 * t, None)
        return c

    lax.fori_loop(0, before // 2, double, 0)

    @pl.when(before % 2 == 1)
    def _():
        block((before - 1) * t, t, None)

    @pl.when(i > 0)
    def _():
        block((i - 1) * t, 2 * t, t)

    @pl.when(i == 0)
    def _():
        block(0, t, 0)

    a_e = acc_e[...]
    a_o = acc_o[...]
    o_ref[0] = jnp.where(lo, a_e / a_e[:, den_e:den_e + 1], a_o / a_o[:, 0:1]).astype(BF16)


def _attn(qkv, cs, t):
    b, s, _ = qkv.shape
    hp = ATT_WIDTH // LANES
    return pl.pallas_call(
        functools.partial(_attn_kernel, t=t),
        out_shape=jax.ShapeDtypeStruct((b, s, ATT_WIDTH), BF16),
        grid=(b, hp, s // t),
        in_specs=[pl.BlockSpec((1, t, LANES), lambda bi, h, i: (bi, i, h)),
                  pl.BlockSpec((1, s, LANES), lambda bi, h, i: (bi, 0, hp + h)),
                  pl.BlockSpec((1, s, LANES), lambda bi, h, i: (bi, 0, 2 * hp + h)),
                  pl.BlockSpec((1, s, LANES), lambda bi, h, i: (bi, 0, 0))],
        out_specs=pl.BlockSpec((1, t, LANES), lambda bi, h, i: (bi, i, h)),
        scratch_shapes=([pltpu.VMEM((s, LANES), BF16)] * 4 + [pltpu.VMEM((t, LANES), F32)] * 2
                        + [pltpu.VMEM((t, LANES), F32)] * 2),
        compiler_params=pltpu.CompilerParams(
            dimension_semantics=("arbitrary", "arbitrary", "arbitrary"),
            vmem_limit_bytes=VMEM_LIMIT),
        name="attn",
    )(qkv, qkv, qkv, cs)


def _out_route_kernel(*refs):
    xs_ref = refs[8]
    last = pl.num_programs(0) - 1
    pl.when(pl.program_id(0) < last)(functools.partial(_route_block, *refs))

    @pl.when(pl.program_id(0) == last)
    def _():
        xs_ref[...] = jnp.zeros_like(xs_ref)


def _route_block(x_ref, att_ref, lru_ref, wo_ref, g_ref, wr_ref, br_ref,
                 x1_ref, xs_ref, tok_ref, nch_ref):
    tm = x_ref.shape[0]
    aw = att_ref.shape[1]
    x1 = (x_ref[...]
          + jnp.dot(att_ref[...], wo_ref[:aw, :], preferred_element_type=F32)
          + jnp.dot(lru_ref[...], wo_ref[aw:, :], preferred_element_type=F32))
    x1_ref[...] = x1
    h2 = _rms(x1, g_ref[...])
    h2_hi = h2.astype(BF16)
    h2_lo = (h2 - h2_hi.astype(F32)).astype(BF16)

    logits = (jnp.dot(h2_hi, wr_ref[0], preferred_element_type=F32)
              + jnp.dot(h2_lo, wr_ref[0], preferred_element_type=F32)
              + jnp.dot(h2_hi, wr_ref[1], preferred_element_type=F32)) + br_ref[...]
    lane = lax.broadcasted_iota(jnp.int32, (tm, LANES), 1)
    gl = jnp.where(lane < N_GROUPS, logits, MASK_VALUE)
    gmax = jnp.max(gl, axis=-1, keepdims=True)
    gsum = jnp.sum(jnp.exp(gl - gmax), axis=-1, keepdims=True)
    g_w = 1.0 / gsum
    g_idx = jnp.min(jnp.where(gl == gmax, lane, LANES), axis=-1, keepdims=True)
    in_group = (lane >= N_GROUPS) & (lane < N_GROUPS + N_EXPERTS) & (
        lax.shift_right_logical(lane - N_GROUPS, 3) == g_idx)
    il = jnp.where(in_group, logits, MASK_VALUE)
    m1 = jnp.max(il, axis=-1, keepdims=True)
    e1 = jnp.min(jnp.where(il == m1, lane, LANES), axis=-1, keepdims=True)
    il2 = jnp.where(lane == e1, MASK_VALUE, il)
    m2 = jnp.max(il2, axis=-1, keepdims=True)
    e2 = jnp.min(jnp.where(il2 == m2, lane, LANES), axis=-1, keepdims=True)
    ratio = jnp.exp(m2 - m1)
    w1 = g_w / (1.0 + ratio)
    w2 = w1 * ratio
    hot1 = lane == e1 - N_GROUPS
    hot2 = lane == e2 - N_GROUPS

    onehot = jnp.where(hot1 | hot2, 1.0, 0.0)
    rr = lax.broadcasted_iota(jnp.int32, (tm, tm), 0)
    cc = lax.broadcasted_iota(jnp.int32, (tm, tm), 1)
    tri = jnp.where(cc < rr, 1.0, 0.0).astype(BF16)
    before = jnp.dot(tri, onehot.astype(BF16), preferred_element_type=F32)
    cnt = jnp.sum(onehot, axis=0, keepdims=True)
    nch = jnp.floor((cnt + (CHUNK - 1)) * (1.0 / CHUNK))
    r128 = lax.broadcasted_iota(jnp.int32, (LANES, LANES), 0)
    c128 = lax.broadcasted_iota(jnp.int32, (LANES, LANES), 1)
    upper = jnp.where(r128 < c128, 1.0, 0.0).astype(BF16)
    nch8 = jnp.broadcast_to(nch, (SUBLANES, LANES))
    seg_off = CHUNK * jnp.dot(nch8.astype(BF16), upper, preferred_element_type=F32)[0:1, :]
    pos = seg_off + before
    pos1 = jnp.sum(jnp.where(hot1, pos, 0.0), axis=-1, keepdims=True)
    pos2 = jnp.sum(jnp.where(hot2, pos, 0.0), axis=-1, keepdims=True)
    nch_ref[0] = nch8

    tok = jnp.where(lane == 0, pos1, 0.0)
    tok = jnp.where(lane == 1, pos2, tok)
    tok = jnp.where(lane == 2, w1, tok)
    tok = jnp.where(lane == 3, w2, tok)
    tok_ref[...] = tok

    tok_t = jnp.transpose(tok)
    p1 = tok_t[0:1, :].astype(jnp.int32)
    p2 = tok_t[1:2, :].astype(jnp.int32)
    rows = xs_ref.shape[0]
    ri = lax.broadcasted_iota(jnp.int32, (rows, tm), 0)
    perm = jnp.where((ri == p1) | (ri == p2), 1.0, 0.0).astype(BF16)
    xs_ref[...] = jnp.dot(perm, h2_hi, preferred_element_type=F32).astype(BF16)


def _out_route(x2, att, lru, wo, g, wr, br, tm):
    n, d = x2.shape
    aw = att.shape[1]
    nblk = n // tm
    rows = _block_rows(tm)
    blk = lambda i: jnp.minimum(i, nblk - 1)
    row = lambda c: pl.BlockSpec((tm, c), lambda i: (blk(i), 0))
    full = lambda shape: pl.BlockSpec(shape, lambda i: (0,) * len(shape))
    return pl.pallas_call(
        _out_route_kernel,
        out_shape=(jax.ShapeDtypeStruct((n, d), F32),
                   jax.ShapeDtypeStruct(((nblk + 1) * rows, d), BF16),
                   jax.ShapeDtypeStruct((n, LANES), F32),
                   jax.ShapeDtypeStruct((nblk, SUBLANES, LANES), F32)),
        grid=(nblk + 1,),
        in_specs=[row(d), row(aw), row(d - aw), full((d, d)), full((1, d)),
                  full((2, d, LANES)), full((1, LANES))],
        out_specs=(row(d), pl.BlockSpec((rows, d), lambda i: (i, 0)), row(LANES),
                   pl.BlockSpec((1, SUBLANES, LANES), lambda i: (blk(i), 0, 0))),
        compiler_params=pltpu.CompilerParams(
            dimension_semantics=("arbitrary",), vmem_limit_bytes=VMEM_LIMIT),
        name="out_route",
    )(x2, att, lru, wo, g, wr, br)


GATHER_AHEAD = 2
X_SLOTS = GATHER_AHEAD + 1
Y_SLOTS = 2


def _experts_kernel(te_ref, nu_ref, src_ref, dst_ref, xs_hbm, wg_ref, wu_ref, wd_ref,
                    ys_hbm, xbuf, ybuf, wg16, wu16, wd16, gsem, ssem):
    del xs_hbm
    i = pl.program_id(0)
    nu = nu_ref[0]

    def gather(tile, k):
        row = pl.multiple_of(src_ref[tile * TILE_CHUNKS + k], CHUNK)
        slot = tile % X_SLOTS
        return pltpu.make_async_copy(ys_hbm.at[pl.ds(row, CHUNK), :],
                                     xbuf.at[slot, pl.ds(k * CHUNK, CHUNK), :], gsem.at[slot])

    def scatter(tile, k):
        row = pl.multiple_of(dst_ref[tile * TILE_CHUNKS + k], CHUNK)
        slot = tile % Y_SLOTS
        return pltpu.make_async_copy(ybuf.at[slot, pl.ds(k * CHUNK, CHUNK), :],
                                     ys_hbm.at[pl.ds(row, CHUNK), :], ssem.at[slot])

    def each_chunk(copy, tile, op):
        for k in range(TILE_CHUNKS):
            getattr(copy(tile, k), op)()

    @pl.when((i == 0) & (nu > 0))
    def _():
        for ahead in range(GATHER_AHEAD):
            each_chunk(gather, ahead, "start")

    @pl.when((i < nu) & ((i == 0) | (te_ref[i] != te_ref[jnp.maximum(i - 1, 0)])))
    def _():
        wg16[...] = wg_ref[0].astype(BF16)
        wu16[...] = wu_ref[0].astype(BF16)
        wd16[...] = wd_ref[0].astype(BF16)

    @pl.when(i < nu)
    def _():
        each_chunk(gather, i, "wait")
        x = xbuf[i % X_SLOTS]
        g = jnp.dot(x, wg16[...], preferred_element_type=F32)
        u = jnp.dot(x, wu16[...], preferred_element_type=F32)
        h = (g * _sigmoid(g) * u).astype(BF16)
        ybuf[i % Y_SLOTS] = jnp.dot(h, wd16[...], preferred_element_type=F32).astype(BF16)
        each_chunk(scatter, i, "start")
        each_chunk(gather, i + GATHER_AHEAD, "start")

        @pl.when(i > 0)
        def _():
            each_chunk(scatter, i - 1, "wait")

        @pl.when(i == nu - 1)
        def _():
            for ahead in range(1, GATHER_AHEAD + 1):
                each_chunk(gather, i + ahead, "wait")
            each_chunk(scatter, i, "wait")


def _experts(tile_expert, n_used, chunk_src, chunk_dst, xs, wg, wu, wd):
    d = xs.shape[1]
    f = wg.shape[2]
    tm = TILE_CHUNKS * CHUNK
    n_tiles = tile_expert.shape[0]
    wspec = lambda shape: pl.BlockSpec(shape, lambda i, te, nu, src, dst: (te[i], 0, 0))
    grid_spec = pltpu.PrefetchScalarGridSpec(
        num_scalar_prefetch=4,
        grid=(n_tiles,),
        in_specs=[pl.BlockSpec(memory_space=pl.ANY),
                  wspec((1, d, f)), wspec((1, d, f)), wspec((1, f, d))],
        out_specs=pl.BlockSpec(memory_space=pl.ANY),
        scratch_shapes=[pltpu.VMEM((X_SLOTS, tm, d), BF16), pltpu.VMEM((Y_SLOTS, tm, d), BF16),
                        pltpu.VMEM((d, f), BF16), pltpu.VMEM((d, f), BF16), pltpu.VMEM((f, d), BF16),
                        pltpu.SemaphoreType.DMA((X_SLOTS,)), pltpu.SemaphoreType.DMA((Y_SLOTS,))],
    )
    return pl.pallas_call(
        _experts_kernel,
        out_shape=jax.ShapeDtypeStruct(xs.shape, BF16),
        grid_spec=grid_spec,
        input_output_aliases={4: 0},
        compiler_params=pltpu.CompilerParams(
            dimension_semantics=("arbitrary",), vmem_limit_bytes=VMEM_LIMIT),
        name="experts",
    )(tile_expert, n_used, chunk_src, chunk_dst, xs, wg, wu, wd)


def _combine_kernel(x1_ref, tok_ref, g_ref, ys_ref, o_ref):
    tm = x1_ref.shape[0]
    rows = ys_ref.shape[0]
    tok = tok_ref[...]
    pos1 = tok[:, 0:1].astype(jnp.int32)
    pos2 = tok[:, 1:2].astype(jnp.int32)
    ci = lax.broadcasted_iota(jnp.int32, (tm, rows), 1)
    unperm = jnp.where(ci == pos1, tok[:, 2:3], jnp.where(ci == pos2, tok[:, 3:4], 0.0))
    y = jnp.dot(unperm.astype(BF16), ys_ref[...], preferred_element_type=F32)
    o_ref[...] = _rms(x1_ref[...] + y, g_ref[...])


def _combine(x1, tok, g, ys, tm):
    n, d = x1.shape
    rows = _block_rows(tm)
    return pl.pallas_call(
        _combine_kernel,
        out_shape=jax.ShapeDtypeStruct((n, d), F32),
        grid=(n // tm,),
        in_specs=[pl.BlockSpec((tm, d), lambda i: (i, 0)),
                  pl.BlockSpec((tm, LANES), lambda i: (i, 0)),
                  pl.BlockSpec((1, d), lambda i: (0, 0)),
                  pl.BlockSpec((rows, d), lambda i: (i, 0))],
        out_specs=pl.BlockSpec((tm, d), lambda i: (i, 0)),
        compiler_params=pltpu.CompilerParams(
            dimension_semantics=("arbitrary",), vmem_limit_bytes=VMEM_LIMIT),
        name="combine",
    )(x1, tok, g, ys)


def _block_diag(w):
    nb, bd, _ = w.shape
    eye = jnp.eye(nb, dtype=w.dtype)
    return (eye[:, None, :, None] * w[:, :, None, :]).reshape(nb * bd, nb * bd)


def _tile_tables(nch, rows):
    nblk = nch.shape[0]
    ex_cum = lambda a, axis: jnp.cumsum(a, axis=axis) - a
    ch_off = ex_cum(nch, 1)
    blk_cum = ex_cum(nch, 0)
    tot = jnp.sum(nch, axis=0)
    tiles_e = (tot + TILE_CHUNKS - 1) // TILE_CHUNKS
    t_end = jnp.cumsum(tiles_e)
    t_start = t_end - tiles_e
    max_chunks = nblk * ((rows - N_EXPERTS * CHUNK) // CHUNK + N_EXPERTS)
    n_tiles = pl.cdiv(max_chunks, TILE_CHUNKS) + N_EXPERTS
    tid = jnp.arange(n_tiles + GATHER_AHEAD, dtype=jnp.int32)
    slot = jnp.arange(TILE_CHUNKS, dtype=jnp.int32)[None, :]
    te = jnp.minimum(jnp.sum((tid[:, None] >= t_end[None, :]).astype(jnp.int32), axis=1),
                     N_EXPERTS - 1)
    is_te = te[:, None] == jnp.arange(N_EXPERTS, dtype=jnp.int32)[None, :]
    of_te = lambda v: jnp.sum(jnp.where(is_te[(...,) + (None,) * (v.ndim - 1)], v[None], 0), axis=1)
    q = (tid - of_te(t_start))[:, None] * TILE_CHUNKS + slot
    valid = (tid < t_end[-1])[:, None] & (q < of_te(tot)[:, None])
    seg_start = of_te(blk_cum.T)
    seg_end = seg_start + of_te(nch.T)
    bidx = jnp.minimum(jnp.sum((q[:, :, None] >= seg_end[:, None, :]).astype(jnp.int32), axis=-1),
                       nblk - 1)
    is_b = bidx[:, :, None] == jnp.arange(nblk, dtype=jnp.int32)[None, None, :]
    of_b = lambda v: jnp.sum(jnp.where(is_b, v[:, None, :], 0), axis=-1)
    row = bidx * rows + CHUNK * (of_b(of_te(ch_off.T)) + q - of_b(seg_start))
    src = jnp.where(valid, row, rows - CHUNK).astype(jnp.int32)
    assert Y_SLOTS * TILE_CHUNKS * CHUNK <= rows
    spare = nblk * rows + CHUNK * ((tid % Y_SLOTS)[:, None] * TILE_CHUNKS + slot)
    dst = jnp.where(valid, row, spare).astype(jnp.int32)
    return (te[:n_tiles], t_end[-1:].astype(jnp.int32), src.reshape(-1),
            dst[:n_tiles].reshape(-1))


def kernel(x, mix_norm, w_in, b_forget, conv_w, conv_b, w_a, b_a, w_x, b_x, lru_lambda, w_out,
           ffn_norm, w_group, b_group, w_inner, b_inner, w_gate, w_up, w_down, final_norm):
    b, s, d = x.shape
    assert w_in.shape[0] == 1, "the final rmsnorm is fused into the single layer's combine stage"
    n = b * s
    tm = min(512, s)
    x2 = x.reshape(n, d)

    sizes = (ATT_WIDTH, ATT_WIDTH, ATT_WIDTH, ATT_HEADS, LRU_WIDTH, LRU_WIDTH)
    o = [0]
    for v in sizes:
        o.append(o[-1] + v)
    w_f = jnp.pad(w_in[0][:, o[3]:o[4]], ((0, 0), (0, LANES - ATT_HEADS)))
    w_cat = jnp.concatenate([w_in[0][:, :o[3]], w_in[0][:, o[4]:], w_f], axis=1).astype(BF16)
    bfg = jnp.pad(b_forget[0], (0, LANES - ATT_HEADS)).reshape(1, LANES)
    qkv, lru, cs = _mix_in(x, mix_norm.reshape(1, d), w_cat,
                           conv_w[0], conv_b.reshape(1, -1),
                           _block_diag(w_a[0]).astype(BF16), b_a.reshape(1, -1),
                           _block_diag(w_x[0]).astype(BF16), b_x.reshape(1, -1),
                           lru_lambda.reshape(1, -1), bfg, tm)

    att = _attn(qkv, cs, tm)

    wr = jnp.concatenate(
        [w_group[0], jnp.transpose(w_inner[0], (1, 0, 2)).reshape(d, N_EXPERTS)], axis=1)
    wr = jnp.pad(wr, ((0, 0), (0, LANES - wr.shape[1])))
    wr_hi = wr.astype(BF16)
    wr = jnp.stack([wr_hi, (wr - wr_hi.astype(F32)).astype(BF16)])
    br = jnp.concatenate([b_group[0], b_inner[0].reshape(-1)])
    br = jnp.pad(br, (0, LANES - br.shape[0])).reshape(1, LANES)
    x1, xs, tok, nch = _out_route(x2, att.reshape(n, -1), lru.reshape(n, -1),
                                  w_out[0].astype(BF16), ffn_norm.reshape(1, d), wr, br, tm)

    rows = _block_rows(tm)
    tables = _tile_tables(nch[:, 0, :N_EXPERTS].astype(jnp.int32), rows)
    fe = w_gate.shape[-1]
    ys = _experts(*tables, xs, w_gate.reshape(N_EXPERTS, d, fe), w_up.reshape(N_EXPERTS, d, fe),
                  w_down.reshape(N_EXPERTS, fe, d))
    out = _combine(x1, tok, final_norm.reshape(1, d), ys, tm)
    return out.reshape(b, s, d)
```

```python
import functools

import jax
import jax.numpy as jnp
from jax import lax
from jax.experimental import pallas as pl
from jax.experimental.pallas import tpu as pltpu

ATT_HEADS = 8
HEAD_DIM = 64
ATT_WIDTH = ATT_HEADS * HEAD_DIM
LRU_WIDTH = 512
LRU_BLOCKS = 8
CONV_WIDTH = 4
LRU_C = 8.0
N_GROUPS = 4
EXPERTS_PER_GROUP = 8
N_EXPERTS = N_GROUPS * EXPERTS_PER_GROUP
D_EXPERT = 256
RMS_EPS = 1e-6
LANES = 128
SUBLANES = 8
CHUNK = 16
TILE_CHUNKS = 16
MASK_VALUE = -1e30
LOG2E = 1.4426950408889634
DECAY_PARTS = 3
VMEM_LIMIT = 48 * 1024 * 1024

BF16 = jnp.bfloat16
F32 = jnp.float32


def _rms(x, g):
    inv = lax.rsqrt(jnp.mean(x * x, axis=-1, keepdims=True) + RMS_EPS)
    return x * inv * g


def _block_rows(tm):
    return 2 * tm + N_EXPERTS * CHUNK


def _sigmoid(v):
    return 0.5 * jnp.tanh(0.5 * v) + 0.5


def _scan_rows(a, u, carry):
    ts, c = u.shape
    g = ts // SUBLANES
    u3 = u.reshape(g, SUBLANES, c)
    a3 = None if a is None else a.reshape(g, SUBLANES, c)
    sub = lax.broadcasted_iota(jnp.int32, u3.shape, 1)
    d = 1
    while d < SUBLANES:
        keep = sub >= d
        prev_u = jnp.where(keep, pltpu.roll(u3, d, 1), 0.0)
        if a3 is None:
            u3 = u3 + prev_u
        else:
            u3 = a3 * prev_u + u3
            a3 = a3 * jnp.where(keep, pltpu.roll(a3, d, 1), 1.0)
        d *= 2
    h_in = jnp.broadcast_to(carry, (SUBLANES, c))
    out = []
    for k in range(g):
        hk = u3[k] + (h_in if a3 is None else a3[k] * h_in)
        out.append(hk)
        h_in = jnp.broadcast_to(hk[SUBLANES - 1:, :], (SUBLANES, c))
    return jnp.concatenate(out, axis=0)


def _mix_in_kernel(x_ref, g_ref, w_ref, cw_ref, cb_ref, wa_ref, ba_ref, wx_ref, bx_ref, lam_ref,
                   bf_ref, qkv_ref, lru_ref, cs_ref, xprev, hprev, cprev):
    @pl.when(pl.program_id(1) == 0)
    def _():
        xprev[...] = jnp.zeros_like(xprev)
        hprev[...] = jnp.zeros_like(hprev)
        cprev[...] = jnp.zeros_like(cprev)

    ts = x_ref.shape[1]
    w = LRU_WIDTH
    aw = ATT_WIDTH
    nq = 3 * aw
    hn = _rms(x_ref[0], g_ref[...]).astype(BF16)
    gate = jnp.dot(hn, w_ref[:, nq:nq + w], preferred_element_type=F32)
    rec = jnp.dot(hn, w_ref[:, nq + w:nq + 2 * w], preferred_element_type=F32)
    f_logit = jnp.dot(hn, w_ref[:, nq + 2 * w:], preferred_element_type=F32)
    q = jnp.dot(hn, w_ref[:, :aw], preferred_element_type=F32) * (HEAD_DIM ** -0.5 * LOG2E)
    qkv_ref[0, :, :aw] = q.astype(BF16)
    qkv_ref[0, :, aw:] = jnp.dot(hn, w_ref[:, aw:nq], preferred_element_type=F32).astype(BF16)
    row8 = lax.broadcasted_iota(jnp.int32, (SUBLANES, w), 0)

    prev = xprev[...]
    conv = cb_ref[...] + cw_ref[CONV_WIDTH - 1:CONV_WIDTH, :] * rec
    for k in range(1, CONV_WIDTH):
        rolled = pltpu.roll(rec, k, 0)
        head = jnp.where(row8 < k, pltpu.roll(prev, k, 0), rolled[:SUBLANES])
        shifted = jnp.concatenate([head, rolled[SUBLANES:]], axis=0)
        conv = conv + cw_ref[CONV_WIDTH - 1 - k:CONV_WIDTH - k, :] * shifted
    xprev[...] = rec[ts - SUBLANES:, :]

    cb16 = conv.astype(BF16)
    r = _sigmoid(jnp.dot(cb16, wa_ref[...], preferred_element_type=F32) + ba_ref[...])
    i = _sigmoid(jnp.dot(cb16, wx_ref[...], preferred_element_type=F32) + bx_ref[...])
    log_a = (-LRU_C) * r * jax.nn.softplus(-lam_ref[...])
    a = jnp.exp(log_a)
    th = jnp.tanh(log_a)
    one_minus_a2 = -2.0 * th / (1.0 - th)
    mult = jnp.where(one_minus_a2 > 0.0, one_minus_a2 * lax.rsqrt(one_minus_a2), 0.0)
    u = mult * (i * conv)

    h = _scan_rows(a, u, hprev[...])
    hprev[...] = h[ts - 1:, :]
    lru_ref[0] = (h * jax.nn.gelu(gate, approximate=True)).astype(BF16)

    c = _scan_rows(None, jax.nn.log_sigmoid(f_logit + bf_ref[...]), cprev[...])
    lanec = lax.broadcasted_iota(jnp.int32, (ts, LANES), 1)
    cprev[...] = c[ts - 1:, :]
    rest = c * (-LOG2E)
    parts = jnp.zeros_like(rest)
    for j in range(DECAY_PARTS):
        piece = rest.astype(BF16).astype(F32)
        rest = rest - piece
        shifted = piece if j == 0 else pltpu.roll(piece, ATT_HEADS * j, 1)
        parts = jnp.where((lanec >= ATT_HEADS * j) & (lanec < ATT_HEADS * (j + 1)), shifted, parts)
    cs_ref[0] = parts.astype(BF16)


def _mix_in(x, g, w_cat, cw, cb, wa, ba, wx, bx, lam, bfg, ts):
    b, s, d = x.shape
    w = LRU_WIDTH
    full = lambda shape: pl.BlockSpec(shape, lambda bi, ti: (0,) * len(shape))
    tile = lambda c: pl.BlockSpec((1, ts, c), lambda bi, ti: (bi, ti, 0))
    return pl.pallas_call(
        _mix_in_kernel,
        out_shape=(jax.ShapeDtypeStruct((b, s, 3 * ATT_WIDTH), BF16),
                   jax.ShapeDtypeStruct((b, s, w), BF16),
                   jax.ShapeDtypeStruct((b, s, LANES), BF16)),
        grid=(b, s // ts),
        in_specs=[tile(d), full((1, d)), full(w_cat.shape),
                  full((CONV_WIDTH, w)), full((1, w)),
                  full((w, w)), full((1, w)), full((w, w)), full((1, w)),
                  full((1, w)), full((1, LANES))],
        out_specs=(tile(3 * ATT_WIDTH), tile(w), tile(LANES)),
        scratch_shapes=[pltpu.VMEM((SUBLANES, w), F32),
                        pltpu.VMEM((1, w), F32),
                        pltpu.VMEM((1, LANES), F32)],
        compiler_params=pltpu.CompilerParams(
            dimension_semantics=("arbitrary", "arbitrary"), vmem_limit_bytes=VMEM_LIMIT),
        name="mix_in",
    )(x, g, w_cat, cw, cb, wa, ba, wx, bx, lam, bfg)


def _attn_kernel(q_ref, k_ref, v_ref, cs_ref, o_ref, ka_e, ka_o, va_e, va_o,
                 m_e, m_o, acc_e, acc_o, *, t):
    head0 = 2 * pl.program_id(1)
    i = pl.program_id(2)
    lane = lax.broadcasted_iota(jnp.int32, (1, LANES), 1)
    lo = lane < HEAD_DIM
    den_e = HEAD_DIM

    def ones_where(cond, rows):
        return jnp.broadcast_to(jnp.where(cond, 1.0, 0.0), (rows, LANES)).astype(BF16)

    @pl.when(i == 0)
    def _():
        r = lax.broadcasted_iota(jnp.int32, (LANES, LANES), 0)
        c = lax.broadcasted_iota(jnp.int32, (LANES, LANES), 1)

        def pieces(head, base):
            j = c - base
            sel = jnp.where((j >= 0) & (j < DECAY_PARTS) & (r == head + ATT_HEADS * j), 1.0, 0.0)
            return jnp.dot(cs_ref[0], sel.astype(BF16), preferred_element_type=F32).astype(BF16)

        k2 = k_ref[0]
        v2 = v_ref[0]
        ka_e[...] = jnp.where(lo, k2, pieces(head0, HEAD_DIM))
        ka_o[...] = jnp.where(lo, pieces(head0 + 1, 0), k2)
        va_e[...] = jnp.where(lo, v2, ones_where(lane == den_e, v2.shape[0]))
        va_o[...] = jnp.where(lo, ones_where(lane == 0, v2.shape[0]), v2)

    q = q_ref[0]
    qa_e = jnp.where(lo, q, ones_where(lane < HEAD_DIM + DECAY_PARTS, t))
    qa_o = jnp.where(lo, ones_where(lane < DECAY_PARTS, t), q)
    nt = (((1,), (1,)), ((), ()))

    heads = ((qa_e, ka_e, va_e, m_e, acc_e), (qa_o, ka_o, va_o, m_o, acc_o))
    for _, _, _, m_ref, acc_ref in heads:
        m_ref[...] = jnp.full(m_ref.shape, MASK_VALUE, F32)
        acc_ref[...] = jnp.zeros(acc_ref.shape, F32)

    def block(row0, nrows, key0, width, key_rel):
        key0 = pl.multiple_of(key0, t // 2)
        rows = slice(row0, row0 + nrows)
        if key_rel is not None:
            rr = lax.broadcasted_iota(jnp.int32, (nrows, width), 0) + row0
            cc = lax.broadcasted_iota(jnp.int32, (nrows, width), 1) + key_rel
            keep = cc <= rr
        updates = []
        for qa, ka, va, m_ref, acc_ref in heads:
            s = lax.dot_general(qa[rows], ka[pl.ds(key0, width), :], nt,
                                preferred_element_type=F32)
            if key_rel is not None:
                s = jnp.where(keep, s, MASK_VALUE)
            cols = [s[:, c * LANES:(c + 1) * LANES] for c in range(width // LANES)]
            m_old = m_ref[rows, :]
            row_max = jnp.max(functools.reduce(jnp.maximum, cols), axis=-1, keepdims=True)
            m_new = jnp.maximum(m_old, row_max)
            p = jnp.concatenate([jnp.exp2(c - m_new) for c in cols], axis=1).astype(BF16)
            acc = acc_ref[rows, :] * jnp.exp2(m_old - m_new) + jnp.dot(
                p, va[pl.ds(key0, width), :], preferred_element_type=F32)
            updates.append((m_ref, m_new, acc_ref, acc))
        for m_ref, m_new, acc_ref, acc in updates:
            m_ref[rows, :] = m_new
            acc_ref[rows, :] = acc

    def full(j, c):
        block(0, t, j * t, t, None)
        return c

    lax.fori_loop(0, i, full, 0)
    half = t // 2
    block(0, t, i * t, half, 0)
    block(half, half, i * t + half, half, half)

    a_e = acc_e[...]
    a_o = acc_o[...]
    o_ref[0] = jnp.where(lo, a_e / a_e[:, den_e:den_e + 1], a_o / a_o[:, 0:1]).astype(BF16)


def _attn(qkv, cs, t):
    b, s, _ = qkv.shape
    hp = ATT_WIDTH // LANES
    return pl.pallas_call(
        functools.partial(_attn_kernel, t=t),
        out_shape=jax.ShapeDtypeStruct((b, s, ATT_WIDTH), BF16),
        grid=(b, hp, s // t),
        in_specs=[pl.BlockSpec((1, t, LANES), lambda bi, h, i: (bi, i, h)),
                  pl.BlockSpec((1, s, LANES), lambda bi, h, i: (bi, 0, hp + h)),
                  pl.BlockSpec((1, s, LANES), lambda bi, h, i: (bi, 0, 2 * hp + h)),
                  pl.BlockSpec((1, s, LANES), lambda bi, h, i: (bi, 0, 0))],
        out_specs=pl.BlockSpec((1, t, LANES), lambda bi, h, i: (bi, i, h)),
        scratch_shapes=([pltpu.VMEM((s, LANES), BF16)] * 4 + [pltpu.VMEM((t, LANES), F32)] * 2
                        + [pltpu.VMEM((t, LANES), F32)] * 2),
        compiler_params=pltpu.CompilerParams(
            dimension_semantics=("arbitrary", "arbitrary", "arbitrary"),
            vmem_limit_bytes=VMEM_LIMIT),
        name="attn",
    )(qkv, qkv, qkv, cs)


def _out_route_kernel(*refs):
    xs_ref = refs[8]
    last = pl.num_programs(0) - 1
    pl.when(pl.program_id(0) < last)(functools.partial(_route_block, *refs))

    @pl.when(pl.program_id(0) == last)
    def _():
        xs_ref[...] = jnp.zeros_like(xs_ref)


def _route_block(x_ref, att_ref, lru_ref, wo_ref, g_ref, wr_ref, br_ref,
                 x1_ref, xs_ref, tok_ref, nch_ref):
    tm = x_ref.shape[0]
    aw = att_ref.shape[1]
    x1 = (x_ref[...]
          + jnp.dot(att_ref[...], wo_ref[:aw, :], preferred_element_type=F32)
          + jnp.dot(lru_ref[...], wo_ref[aw:, :], preferred_element_type=F32))
    x1_ref[...] = x1
    h2 = _rms(x1, g_ref[...])
    h2_hi = h2.astype(BF16)
    h2_lo = (h2 - h2_hi.astype(F32)).astype(BF16)

    logits = (jnp.dot(h2_hi, wr_ref[0], preferred_element_type=F32)
              + jnp.dot(h2_lo, wr_ref[0], preferred_element_type=F32)
              + jnp.dot(h2_hi, wr_ref[1], preferred_element_type=F32)) + br_ref[...]
    lane = lax.broadcasted_iota(jnp.int32, (tm, LANES), 1)
    gl = jnp.where(lane < N_GROUPS, logits, MASK_VALUE)
    gmax = jnp.max(gl, axis=-1, keepdims=True)
    gsum = jnp.sum(jnp.exp(gl - gmax), axis=-1, keepdims=True)
    g_w = 1.0 / gsum
    g_idx = jnp.min(jnp.where(gl == gmax, lane, LANES), axis=-1, keepdims=True)
    in_group = (lane >= N_GROUPS) & (lane < N_GROUPS + N_EXPERTS) & (
        lax.shift_right_logical(lane - N_GROUPS, 3) == g_idx)
    il = jnp.where(in_group, logits, MASK_VALUE)
    m1 = jnp.max(il, axis=-1, keepdims=True)
    e1 = jnp.min(jnp.where(il == m1, lane, LANES), axis=-1, keepdims=True)
    il2 = jnp.where(lane == e1, MASK_VALUE, il)
    m2 = jnp.max(il2, axis=-1, keepdims=True)
    e2 = jnp.min(jnp.where(il2 == m2, lane, LANES), axis=-1, keepdims=True)
    ratio = jnp.exp(m2 - m1)
    w1 = g_w / (1.0 + ratio)
    w2 = w1 * ratio
    hot1 = lane == e1 - N_GROUPS
    hot2 = lane == e2 - N_GROUPS

    onehot = jnp.where(hot1 | hot2, 1.0, 0.0)
    rr = lax.broadcasted_iota(jnp.int32, (tm, tm), 0)
    cc = lax.broadcasted_iota(jnp.int32, (tm, tm), 1)
    tri = jnp.where(cc < rr, 1.0, 0.0).astype(BF16)
    before = jnp.dot(tri, onehot.astype(BF16), preferred_element_type=F32)
    cnt = jnp.sum(onehot, axis=0, keepdims=True)
    nch = jnp.floor((cnt + (CHUNK - 1)) * (1.0 / CHUNK))
    r128 = lax.broadcasted_iota(jnp.int32, (LANES, LANES), 0)
    c128 = lax.broadcasted_iota(jnp.int32, (LANES, LANES), 1)
    upper = jnp.where(r128 < c128, 1.0, 0.0).astype(BF16)
    nch8 = jnp.broadcast_to(nch, (SUBLANES, LANES))
    seg_off = CHUNK * jnp.dot(nch8.astype(BF16), upper, preferred_element_type=F32)[0:1, :]
    pos = seg_off + before
    pos1 = jnp.sum(jnp.where(hot1, pos, 0.0), axis=-1, keepdims=True)
    pos2 = jnp.sum(jnp.where(hot2, pos, 0.0), axis=-1, keepdims=True)
    nch_ref[0] = nch8

    tok = jnp.where(lane == 0, pos1, 0.0)
    tok = jnp.where(lane == 1, pos2, tok)
    tok = jnp.where(lane == 2, w1, tok)
    tok = jnp.where(lane == 3, w2, tok)
    tok_ref[...] = tok

    tok_t = jnp.transpose(tok)
    p1 = tok_t[0:1, :].astype(jnp.int32)
    p2 = tok_t[1:2, :].astype(jnp.int32)
    rows = xs_ref.shape[0]
    ri = lax.broadcasted_iota(jnp.int32, (rows, tm), 0)
    perm = jnp.where((ri == p1) | (ri == p2), 1.0, 0.0).astype(BF16)
    xs_ref[...] = jnp.dot(perm, h2_hi, preferred_element_type=F32).astype(BF16)


def _out_route(x2, att, lru, wo, g, wr, br, tm):
    n, d = x2.shape
    aw = att.shape[1]
    nblk = n // tm
    rows = _block_rows(tm)
    blk = lambda i: jnp.minimum(i, nblk - 1)
    row = lambda c: pl.BlockSpec((tm, c), lambda i: (blk(i), 0))
    full = lambda shape: pl.BlockSpec(shape, lambda i: (0,) * len(shape))
    return pl.pallas_call(
        _out_route_kernel,
        out_shape=(jax.ShapeDtypeStruct((n, d), F32),
                   jax.ShapeDtypeStruct(((nblk + 1) * rows, d), BF16),
                   jax.ShapeDtypeStruct((n, LANES), F32),
                   jax.ShapeDtypeStruct((nblk, SUBLANES, LANES), F32)),
        grid=(nblk + 1,),
        in_specs=[row(d), row(aw), row(d - aw), full((d, d)), full((1, d)),
                  full((2, d, LANES)), full((1, LANES))],
        out_specs=(row(d), pl.BlockSpec((rows, d), lambda i: (i, 0)), row(LANES),
                   pl.BlockSpec((1, SUBLANES, LANES), lambda i: (blk(i), 0, 0))),
        compiler_params=pltpu.CompilerParams(
            dimension_semantics=("arbitrary",), vmem_limit_bytes=VMEM_LIMIT),
        name="out_route",
    )(x2, att, lru, wo, g, wr, br)


GATHER_AHEAD = 2
X_SLOTS = GATHER_AHEAD + 1
Y_SLOTS = 2


def _experts_kernel(te_ref, nu_ref, src_ref, dst_ref, xs_hbm, wg_ref, wu_ref, wd_ref,
                    ys_hbm, xbuf, ybuf, wg16, wu16, wd16, gsem, ssem):
    del xs_hbm
    i = pl.program_id(0)
    nu = nu_ref[0]

    def gather(tile, k):
        row = pl.multiple_of(src_ref[tile * TILE_CHUNKS + k], CHUNK)
        slot = tile % X_SLOTS
        return pltpu.make_async_copy(ys_hbm.at[pl.ds(row, CHUNK), :],
                                     xbuf.at[slot, pl.ds(k * CHUNK, CHUNK), :], gsem.at[slot])

    def scatter(tile, k):
        row = pl.multiple_of(dst_ref[tile * TILE_CHUNKS + k], CHUNK)
        slot = tile % Y_SLOTS
        return pltpu.make_async_copy(ybuf.at[slot, pl.ds(k * CHUNK, CHUNK), :],
                                     ys_hbm.at[pl.ds(row, CHUNK), :], ssem.at[slot])

    def each_chunk(copy, tile, op):
        for k in range(TILE_CHUNKS):
            getattr(copy(tile, k), op)()

    @pl.when((i == 0) & (nu > 0))
    def _():
        for ahead in range(GATHER_AHEAD):
            each_chunk(gather, ahead, "start")

    @pl.when((i < nu) & ((i == 0) | (te_ref[i] != te_ref[jnp.maximum(i - 1, 0)])))
    def _():
        wg16[...] = wg_ref[0].astype(BF16)
        wu16[...] = wu_ref[0].astype(BF16)
        wd16[...] = wd_ref[0].astype(BF16)

    @pl.when(i < nu)
    def _():
        each_chunk(gather, i, "wait")
        x = xbuf[i % X_SLOTS]
        g = jnp.dot(x, wg16[...], preferred_element_type=F32)
        u = jnp.dot(x, wu16[...], preferred_element_type=F32)
        h = (g * _sigmoid(g) * u).astype(BF16)
        ybuf[i % Y_SLOTS] = jnp.dot(h, wd16[...], preferred_element_type=F32).astype(BF16)
        each_chunk(scatter, i, "start")
        each_chunk(gather, i + GATHER_AHEAD, "start")

        @pl.when(i > 0)
        def _():
            each_chunk(scatter, i - 1, "wait")

        @pl.when(i == nu - 1)
        def _():
            for ahead in range(1, GATHER_AHEAD + 1):
                each_chunk(gather, i + ahead, "wait")
            each_chunk(scatter, i, "wait")


def _experts(tile_expert, n_used, chunk_src, chunk_dst, xs, wg, wu, wd):
    d = xs.shape[1]
    f = wg.shape[2]
    tm = TILE_CHUNKS * CHUNK
    n_tiles = tile_expert.shape[0]
    wspec = lambda shape: pl.BlockSpec(shape, lambda i, te, nu, src, dst: (te[i], 0, 0))
    grid_spec = pltpu.PrefetchScalarGridSpec(
        num_scalar_prefetch=4,
        grid=(n_tiles,),
        in_specs=[pl.BlockSpec(memory_space=pl.ANY),
                  wspec((1, d, f)), wspec((1, d, f)), wspec((1, f, d))],
        out_specs=pl.BlockSpec(memory_space=pl.ANY),
        scratch_shapes=[pltpu.VMEM((X_SLOTS, tm, d), BF16), pltpu.VMEM((Y_SLOTS, tm, d), BF16),
                        pltpu.VMEM((d, f), BF16), pltpu.VMEM((d, f), BF16), pltpu.VMEM((f, d), BF16),
                        pltpu.SemaphoreType.DMA((X_SLOTS,)), pltpu.SemaphoreType.DMA((Y_SLOTS,))],
    )
    return pl.pallas_call(
        _experts_kernel,
        out_shape=jax.ShapeDtypeStruct(xs.shape, BF16),
        grid_spec=grid_spec,
        input_output_aliases={4: 0},
        compiler_params=pltpu.CompilerParams(
            dimension_semantics=("arbitrary",), vmem_limit_bytes=VMEM_LIMIT),
        name="experts",
    )(tile_expert, n_used, chunk_src, chunk_dst, xs, wg, wu, wd)


def _combine_kernel(x1_ref, tok_ref, g_ref, ys_ref, o_ref):
    tm = x1_ref.shape[0]
    rows = ys_ref.shape[0]
    tok = tok_ref[...]
    pos1 = tok[:, 0:1].astype(jnp.int32)
    pos2 = tok[:, 1:2].astype(jnp.int32)
    ci = lax.broadcasted_iota(jnp.int32, (tm, rows), 1)
    unperm = jnp.where(ci == pos1, tok[:, 2:3], jnp.where(ci == pos2, tok[:, 3:4], 0.0))
    y = jnp.dot(unperm.astype(BF16), ys_ref[...], preferred_element_type=F32)
    o_ref[...] = _rms(x1_ref[...] + y, g_ref[...])


def _combine(x1, tok, g, ys, tm):
    n, d = x1.shape
    rows = _block_rows(tm)
    return pl.pallas_call(
        _combine_kernel,
        out_shape=jax.ShapeDtypeStruct((n, d), F32),
        grid=(n // tm,),
        in_specs=[pl.BlockSpec((tm, d), lambda i: (i, 0)),
                  pl.BlockSpec((tm, LANES), lambda i: (i, 0)),
                  pl.BlockSpec((1, d), lambda i: (0, 0)),
                  pl.BlockSpec((rows, d), lambda i: (i, 0))],
        out_specs=pl.BlockSpec((tm, d), lambda i: (i, 0)),
        compiler_params=pltpu.CompilerParams(
            dimension_semantics=("arbitrary",), vmem_limit_bytes=VMEM_LIMIT),
        name="combine",
    )(x1, tok, g, ys)


def _block_diag(w):
    nb, bd, _ = w.shape
    eye = jnp.eye(nb, dtype=w.dtype)
    return (eye[:, None, :, None] * w[:, :, None, :]).reshape(nb * bd, nb * bd)


def _tile_tables(nch, rows):
    nblk = nch.shape[0]
    ex_cum = lambda a, axis: jnp.cumsum(a, axis=axis) - a
    ch_off = ex_cum(nch, 1)
    blk_cum = ex_cum(nch, 0)
    tot = jnp.sum(nch, axis=0)
    tiles_e = (tot + TILE_CHUNKS - 1) // TILE_CHUNKS
    t_end = jnp.cumsum(tiles_e)
    t_start = t_end - tiles_e
    max_chunks = nblk * ((rows - N_EXPERTS * CHUNK) // CHUNK + N_EXPERTS)
    n_tiles = pl.cdiv(max_chunks, TILE_CHUNKS) + N_EXPERTS
    tid = jnp.arange(n_tiles + GATHER_AHEAD, dtype=jnp.int32)
    slot = jnp.arange(TILE_CHUNKS, dtype=jnp.int32)[None, :]
    te = jnp.minimum(jnp.sum((tid[:, None] >= t_end[None, :]).astype(jnp.int32), axis=1),
                     N_EXPERTS - 1)
    is_te = te[:, None] == jnp.arange(N_EXPERTS, dtype=jnp.int32)[None, :]
    of_te = lambda v: jnp.sum(jnp.where(is_te[(...,) + (None,) * (v.ndim - 1)], v[None], 0), axis=1)
    q = (tid - of_te(t_start))[:, None] * TILE_CHUNKS + slot
    valid = (tid < t_end[-1])[:, None] & (q < of_te(tot)[:, None])
    seg_start = of_te(blk_cum.T)
    seg_end = seg_start + of_te(nch.T)
    bidx = jnp.minimum(jnp.sum((q[:, :, None] >= seg_end[:, None, :]).astype(jnp.int32), axis=-1),
                       nblk - 1)
    is_b = bidx[:, :, None] == jnp.arange(nblk, dtype=jnp.int32)[None, None, :]
    of_b = lambda v: jnp.sum(jnp.where(is_b, v[:, None, :], 0), axis=-1)
    row = bidx * rows + CHUNK * (of_b(of_te(ch_off.T)) + q - of_b(seg_start))
    src = jnp.where(valid, row, rows - CHUNK).astype(jnp.int32)
    assert Y_SLOTS * TILE_CHUNKS * CHUNK <= rows
    spare = nblk * rows + CHUNK * ((tid % Y_SLOTS)[:, None] * TILE_CHUNKS + slot)
    dst = jnp.where(valid, row, spare).astype(jnp.int32)
    return (te[:n_tiles], t_end[-1:].astype(jnp.int32), src.reshape(-1),
            dst[:n_tiles].reshape(-1))


def kernel(x, mix_norm, w_in, b_forget, conv_w, conv_b, w_a, b_a, w_x, b_x, lru_lambda, w_out,
           ffn_norm, w_group, b_group, w_inner, b_inner, w_gate, w_up, w_down, final_norm):
    b, s, d = x.shape
    assert w_in.shape[0] == 1, "the final rmsnorm is fused into the single layer's combine stage"
    n = b * s
    tm = min(512, s)
    x2 = x.reshape(n, d)

    sizes = (ATT_WIDTH, ATT_WIDTH, ATT_WIDTH, ATT_HEADS, LRU_WIDTH, LRU_WIDTH)
    o = [0]
    for v in sizes:
        o.append(o[-1] + v)
    w_f = jnp.pad(w_in[0][:, o[3]:o[4]], ((0, 0), (0, LANES - ATT_HEADS)))
    w_cat = jnp.concatenate([w_in[0][:, :o[3]], w_in[0][:, o[4]:], w_f], axis=1).astype(BF16)
    bfg = jnp.pad(b_forget[0], (0, LANES - ATT_HEADS)).reshape(1, LANES)
    qkv, lru, cs = _mix_in(x, mix_norm.reshape(1, d), w_cat,
                           conv_w[0], conv_b.reshape(1, -1),
                           _block_diag(w_a[0]).astype(BF16), b_a.reshape(1, -1),
                           _block_diag(w_x[0]).astype(BF16), b_x.reshape(1, -1),
                           lru_lambda.reshape(1, -1), bfg, tm)

    att = _attn(qkv, cs, min(1024, s))

    wr = jnp.concatenate(
        [w_group[0], jnp.transpose(w_inner[0], (1, 0, 2)).reshape(d, N_EXPERTS)], axis=1)
    wr = jnp.pad(wr, ((0, 0), (0, LANES - wr.shape[1])))
    wr_hi = wr.astype(BF16)
    wr = jnp.stack([wr_hi, (wr - wr_hi.astype(F32)).astype(BF16)])
    br = jnp.concatenate([b_group[0], b_inner[0].reshape(-1)])
    br = jnp.pad(br, (0, LANES - br.shape[0])).reshape(1, LANES)
    x1, xs, tok, nch = _out_route(x2, att.reshape(n, -1), lru.reshape(n, -1),
                                  w_out[0].astype(BF16), ffn_norm.reshape(1, d), wr, br, tm)

    rows = _block_rows(tm)
    tables = _tile_tables(nch[:, 0, :N_EXPERTS].astype(jnp.int32), rows)
    fe = w_gate.shape[-1]
    ys = _experts(*tables, xs, w_gate.reshape(N_EXPERTS, d, fe), w_up.reshape(N_EXPERTS, d, fe),
                  w_down.reshape(N_EXPERTS, fe, d))
    out = _combine(x1, tok, final_norm.reshape(1, d), ys, tm)
    return out.reshape(b, s, d)
```

```python
import functools

import jax
import jax.numpy as jnp
from jax import lax
from jax.experimental import pallas as pl
from jax.experimental.pallas import tpu as pltpu

ATT_HEADS = 8
HEAD_DIM = 64
ATT_WIDTH = ATT_HEADS * HEAD_DIM
LRU_WIDTH = 512
LRU_BLOCKS = 8
CONV_WIDTH = 4
LRU_C = 8.0
N_GROUPS = 4
EXPERTS_PER_GROUP = 8
N_EXPERTS = N_GROUPS * EXPERTS_PER_GROUP
D_EXPERT = 256
RMS_EPS = 1e-6
LANES = 128
SUBLANES = 8
CHUNK = 16
TILE_CHUNKS = 16
MASK_VALUE = -1e30
LOG2E = 1.4426950408889634
DECAY_PARTS = 3
VMEM_LIMIT = 48 * 1024 * 1024

BF16 = jnp.bfloat16
F32 = jnp.float32


def _rms(x, g):
    inv = lax.rsqrt(jnp.mean(x * x, axis=-1, keepdims=True) + RMS_EPS)
    return x * inv * g


def _block_rows(tm):
    return 2 * tm + N_EXPERTS * CHUNK


def _sigmoid(v):
    return 0.5 * jnp.tanh(0.5 * v) + 0.5


def _scan_rows(a, u, carry):
    ts, c = u.shape
    g = ts // SUBLANES
    u3 = u.reshape(g, SUBLANES, c)
    a3 = None if a is None else a.reshape(g, SUBLANES, c)
    sub = lax.broadcasted_iota(jnp.int32, u3.shape, 1)
    d = 1
    while d < SUBLANES:
        keep = sub >= d
        prev_u = jnp.where(keep, pltpu.roll(u3, d, 1), 0.0)
        if a3 is None:
            u3 = u3 + prev_u
        else:
            u3 = a3 * prev_u + u3
            a3 = a3 * jnp.where(keep, pltpu.roll(a3, d, 1), 1.0)
        d *= 2
    h_in = jnp.broadcast_to(carry, (SUBLANES, c))
    out = []
    for k in range(g):
        hk = u3[k] + (h_in if a3 is None else a3[k] * h_in)
        out.append(hk)
        h_in = jnp.broadcast_to(hk[SUBLANES - 1:, :], (SUBLANES, c))
    return jnp.concatenate(out, axis=0)


def _mix_in_kernel(x_ref, g_ref, w_ref, cw_ref, cb_ref, wa_ref, ba_ref, wx_ref, bx_ref, lam_ref,
                   bf_ref, qkv_ref, lru_ref, cs_ref, xprev, hprev, cprev):
    @pl.when(pl.program_id(1) == 0)
    def _():
        xprev[...] = jnp.zeros_like(xprev)
        hprev[...] = jnp.zeros_like(hprev)
        cprev[...] = jnp.zeros_like(cprev)

    ts = x_ref.shape[1]
    w = LRU_WIDTH
    aw = ATT_WIDTH
    nq = 3 * aw
    hn = _rms(x_ref[0], g_ref[...]).astype(BF16)
    gate = jnp.dot(hn, w_ref[:, nq:nq + w], preferred_element_type=F32)
    rec = jnp.dot(hn, w_ref[:, nq + w:nq + 2 * w], preferred_element_type=F32)
    f_logit = jnp.dot(hn, w_ref[:, nq + 2 * w:], preferred_element_type=F32)
    q = jnp.dot(hn, w_ref[:, :aw], preferred_element_type=F32) * (HEAD_DIM ** -0.5 * LOG2E)
    qkv_ref[0, :, :aw] = q.astype(BF16)
    qkv_ref[0, :, aw:] = jnp.dot(hn, w_ref[:, aw:nq], preferred_element_type=F32).astype(BF16)
    row8 = lax.broadcasted_iota(jnp.int32, (SUBLANES, w), 0)

    prev = xprev[...]
    conv = cb_ref[...] + cw_ref[CONV_WIDTH - 1:CONV_WIDTH, :] * rec
    for k in range(1, CONV_WIDTH):
        rolled = pltpu.roll(rec, k, 0)
        head = jnp.where(row8 < k, pltpu.roll(prev, k, 0), rolled[:SUBLANES])
        shifted = jnp.concatenate([head, rolled[SUBLANES:]], axis=0)
        conv = conv + cw_ref[CONV_WIDTH - 1 - k:CONV_WIDTH - k, :] * shifted
    xprev[...] = rec[ts - SUBLANES:, :]

    cb16 = conv.astype(BF16)
    r = _sigmoid(jnp.dot(cb16, wa_ref[...], preferred_element_type=F32) + ba_ref[...])
    i = _sigmoid(jnp.dot(cb16, wx_ref[...], preferred_element_type=F32) + bx_ref[...])
    log_a = (-LRU_C) * r * jax.nn.softplus(-lam_ref[...])
    a = jnp.exp(log_a)
    th = jnp.tanh(log_a)
    one_minus_a2 = -2.0 * th / (1.0 - th)
    mult = jnp.where(one_minus_a2 > 0.0, one_minus_a2 * lax.rsqrt(one_minus_a2), 0.0)
    u = mult * (i * conv)

    h = _scan_rows(a, u, hprev[...])
    hprev[...] = h[ts - 1:, :]
    lru_ref[0] = (h * jax.nn.gelu(gate, approximate=True)).astype(BF16)

    c = _scan_rows(None, jax.nn.log_sigmoid(f_logit + bf_ref[...]), cprev[...])
    lanec = lax.broadcasted_iota(jnp.int32, (ts, LANES), 1)
    cprev[...] = c[ts - 1:, :]
    rest = c * (-LOG2E)
    parts = jnp.zeros_like(rest)
    for j in range(DECAY_PARTS):
        piece = rest.astype(BF16).astype(F32)
        rest = rest - piece
        shifted = piece if j == 0 else pltpu.roll(piece, ATT_HEADS * j, 1)
        parts = jnp.where((lanec >= ATT_HEADS * j) & (lanec < ATT_HEADS * (j + 1)), shifted, parts)
    cs_ref[0] = parts.astype(BF16)


def _mix_in(x, g, w_cat, cw, cb, wa, ba, wx, bx, lam, bfg, ts):
    b, s, d = x.shape
    w = LRU_WIDTH
    full = lambda shape: pl.BlockSpec(shape, lambda bi, ti: (0,) * len(shape))
    tile = lambda c: pl.BlockSpec((1, ts, c), lambda bi, ti: (bi, ti, 0))
    return pl.pallas_call(
        _mix_in_kernel,
        out_shape=(jax.ShapeDtypeStruct((b, s, 3 * ATT_WIDTH), BF16),
                   jax.ShapeDtypeStruct((b, s, w), BF16),
                   jax.ShapeDtypeStruct((b, s, LANES), BF16)),
        grid=(b, s // ts),
        in_specs=[tile(d), full((1, d)), full(w_cat.shape),
                  full((CONV_WIDTH, w)), full((1, w)),
                  full((w, w)), full((1, w)), full((w, w)), full((1, w)),
                  full((1, w)), full((1, LANES))],
        out_specs=(tile(3 * ATT_WIDTH), tile(w), tile(LANES)),
        scratch_shapes=[pltpu.VMEM((SUBLANES, w), F32),
                        pltpu.VMEM((1, w), F32),
                        pltpu.VMEM((1, LANES), F32)],
        compiler_params=pltpu.CompilerParams(
            dimension_semantics=("arbitrary", "arbitrary"), vmem_limit_bytes=VMEM_LIMIT),
        name="mix_in",
    )(x, g, w_cat, cw, cb, wa, ba, wx, bx, lam, bfg)


def _attn_kernel(q_ref, k_ref, v_ref, cs_ref, o_ref, ka_e, ka_o, va_e, va_o,
                 m_e, m_o, acc_e, acc_o, *, t):
    head0 = 2 * pl.program_id(1)
    i = pl.program_id(2)
    lane = lax.broadcasted_iota(jnp.int32, (1, LANES), 1)
    lo = lane < HEAD_DIM
    den_e = HEAD_DIM

    def ones_where(cond, rows):
        return jnp.broadcast_to(jnp.where(cond, 1.0, 0.0), (rows, LANES)).astype(BF16)

    @pl.when(i == 0)
    def _():
        r = lax.broadcasted_iota(jnp.int32, (LANES, LANES), 0)
        c = lax.broadcasted_iota(jnp.int32, (LANES, LANES), 1)

        def pieces(head, base):
            j = c - base
            sel = jnp.where((j >= 0) & (j < DECAY_PARTS) & (r == head + ATT_HEADS * j), 1.0, 0.0)
            return jnp.dot(cs_ref[0], sel.astype(BF16), preferred_element_type=F32).astype(BF16)

        k2 = k_ref[0]
        v2 = v_ref[0]
        ka_e[...] = jnp.where(lo, k2, pieces(head0, HEAD_DIM))
        ka_o[...] = jnp.where(lo, pieces(head0 + 1, 0), k2)
        va_e[...] = jnp.where(lo, v2, ones_where(lane == den_e, v2.shape[0]))
        va_o[...] = jnp.where(lo, ones_where(lane == 0, v2.shape[0]), v2)

    q = q_ref[0]
    qa_e = jnp.where(lo, q, ones_where(lane < HEAD_DIM + DECAY_PARTS, t))
    qa_o = jnp.where(lo, ones_where(lane < DECAY_PARTS, t), q)
    nt = (((1,), (1,)), ((), ()))

    heads = ((qa_e, ka_e, va_e, m_e, acc_e), (qa_o, ka_o, va_o, m_o, acc_o))
    for _, _, _, m_ref, acc_ref in heads:
        m_ref[...] = jnp.full(m_ref.shape, MASK_VALUE, F32)
        acc_ref[...] = jnp.zeros(acc_ref.shape, F32)

    def block(row0, nrows, key0, width, key_rel):
        key0 = pl.multiple_of(key0, t // 2)
        rows = slice(row0, row0 + nrows)
        if key_rel is not None:
            rr = lax.broadcasted_iota(jnp.int32, (nrows, width), 0) + row0
            cc = lax.broadcasted_iota(jnp.int32, (nrows, width), 1) + key_rel
            keep = cc <= rr
        updates = []
        for qa, ka, va, m_ref, acc_ref in heads:
            s = lax.dot_general(qa[rows], ka[pl.ds(key0, width), :], nt,
                                preferred_element_type=F32)
            if key_rel is not None:
                s = jnp.where(keep, s, MASK_VALUE)
            cols = [s[:, c * LANES:(c + 1) * LANES] for c in range(width // LANES)]
            m_old = m_ref[rows, :]
            row_max = jnp.max(functools.reduce(jnp.maximum, cols), axis=-1, keepdims=True)
            m_new = jnp.maximum(m_old, row_max)
            p = jnp.concatenate([jnp.exp2(c - m_new) for c in cols], axis=1).astype(BF16)
            acc = acc_ref[rows, :] * jnp.exp2(m_old - m_new) + jnp.dot(
                p, va[pl.ds(key0, width), :], preferred_element_type=F32)
            updates.append((m_ref, m_new, acc_ref, acc))
        for m_ref, m_new, acc_ref, acc in updates:
            m_ref[rows, :] = m_new
            acc_ref[rows, :] = acc

    def full(j, c):
        block(0, t, j * t, t, None)
        return c

    lax.fori_loop(0, i, full, 0)
    half = t // 2
    block(0, t, i * t, half, 0)
    block(half, half, i * t + half, half, half)

    a_e = acc_e[...]
    a_o = acc_o[...]
    o_ref[0] = jnp.where(lo, a_e / a_e[:, den_e:den_e + 1], a_o / a_o[:, 0:1]).astype(BF16)


def _attn(qkv, cs, t):
    b, s, _ = qkv.shape
    hp = ATT_WIDTH // LANES
    return pl.pallas_call(
        functools.partial(_attn_kernel, t=t),
        out_shape=jax.ShapeDtypeStruct((b, s, ATT_WIDTH), BF16),
        grid=(b, hp, s // t),
        in_specs=[pl.BlockSpec((1, t, LANES), lambda bi, h, i: (bi, i, h)),
                  pl.BlockSpec((1, s, LANES), lambda bi, h, i: (bi, 0, hp + h)),
                  pl.BlockSpec((1, s, LANES), lambda bi, h, i: (bi, 0, 2 * hp + h)),
                  pl.BlockSpec((1, s, LANES), lambda bi, h, i: (bi, 0, 0))],
        out_specs=pl.BlockSpec((1, t, LANES), lambda bi, h, i: (bi, i, h)),
        scratch_shapes=([pltpu.VMEM((s, LANES), BF16)] * 4 + [pltpu.VMEM((t, LANES), F32)] * 2
                        + [pltpu.VMEM((t, LANES), F32)] * 2),
        compiler_params=pltpu.CompilerParams(
            dimension_semantics=("arbitrary", "arbitrary", "arbitrary"),
            vmem_limit_bytes=VMEM_LIMIT),
        name="attn",
    )(qkv, qkv, qkv, cs)


def _out_route_kernel(*refs):
    xs_ref = refs[8]
    last = pl.num_programs(0) - 1
    pl.when(pl.program_id(0) < last)(functools.partial(_route_block, *refs))

    @pl.when(pl.program_id(0) == last)
    def _():
        xs_ref[...] = jnp.zeros_like(xs_ref)


def _route_block(x_ref, att_ref, lru_ref, wo_ref, g_ref, wr_ref, br_ref,
                 x1_ref, xs_ref, tok_ref, nch_ref):
    tm = x_ref.shape[0]
    aw = att_ref.shape[1]
    x1 = (x_ref[...]
          + jnp.dot(att_ref[...], wo_ref[:aw, :], preferred_element_type=F32)
          + jnp.dot(lru_ref[...], wo_ref[aw:, :], preferred_element_type=F32))
    x1_ref[...] = x1
    h2 = _rms(x1, g_ref[...])
    h2_hi = h2.astype(BF16)
    h2_lo = (h2 - h2_hi.astype(F32)).astype(BF16)

    logits = (jnp.dot(h2_hi, wr_ref[0], preferred_element_type=F32)
              + jnp.dot(h2_lo, wr_ref[0], preferred_element_type=F32)
              + jnp.dot(h2_hi, wr_ref[1], preferred_element_type=F32)) + br_ref[...]
    lane = lax.broadcasted_iota(jnp.int32, (tm, LANES), 1)
    gl = jnp.where(lane < N_GROUPS, logits, MASK_VALUE)
    gmax = jnp.max(gl, axis=-1, keepdims=True)
    gsum = jnp.sum(jnp.exp(gl - gmax), axis=-1, keepdims=True)
    g_w = 1.0 / gsum
    g_idx = jnp.min(jnp.where(gl == gmax, lane, LANES), axis=-1, keepdims=True)
    in_group = (lane >= N_GROUPS) & (lane < N_GROUPS + N_EXPERTS) & (
        lax.shift_right_logical(lane - N_GROUPS, 3) == g_idx)
    il = jnp.where(in_group, logits, MASK_VALUE)
    m1 = jnp.max(il, axis=-1, keepdims=True)
    e1 = jnp.min(jnp.where(il == m1, lane, LANES), axis=-1, keepdims=True)
    il2 = jnp.where(lane == e1, MASK_VALUE, il)
    m2 = jnp.max(il2, axis=-1, keepdims=True)
    e2 = jnp.min(jnp.where(il2 == m2, lane, LANES), axis=-1, keepdims=True)
    ratio = jnp.exp(m2 - m1)
    w1 = g_w / (1.0 + ratio)
    w2 = w1 * ratio
    hot1 = lane == e1 - N_GROUPS
    hot2 = lane == e2 - N_GROUPS

    onehot = jnp.where(hot1 | hot2, 1.0, 0.0)
    rr = lax.broadcasted_iota(jnp.int32, (tm, tm), 0)
    cc = lax.broadcasted_iota(jnp.int32, (tm, tm), 1)
    tri = jnp.where(cc < rr, 1.0, 0.0).astype(BF16)
    before = jnp.dot(tri, onehot.astype(BF16), preferred_element_type=F32)
    cnt = jnp.sum(onehot, axis=0, keepdims=True)
    nch = jnp.floor((cnt + (CHUNK - 1)) * (1.0 / CHUNK))
    r128 = lax.broadcasted_iota(jnp.int32, (LANES, LANES), 0)
    c128 = lax.broadcasted_iota(jnp.int32, (LANES, LANES), 1)
    upper = jnp.where(r128 < c128, 1.0, 0.0).astype(BF16)
    nch8 = jnp.broadcast_to(nch, (SUBLANES, LANES))
    seg_off = CHUNK * jnp.dot(nch8.astype(BF16), upper, preferred_element_type=F32)[0:1, :]
    pos = seg_off + before
    pos1 = jnp.sum(jnp.where(hot1, pos, 0.0), axis=-1, keepdims=True)
    pos2 = jnp.sum(jnp.where(hot2, pos, 0.0), axis=-1, keepdims=True)
    nch_ref[0] = nch8

    tok = jnp.where(lane == 0, pos1, 0.0)
    tok = jnp.where(lane == 1, pos2, tok)
    tok = jnp.where(lane == 2, w1, tok)
    tok = jnp.where(lane == 3, w2, tok)
    tok_ref[...] = tok

    tok_t = jnp.transpose(tok)
    p1 = tok_t[0:1, :].astype(jnp.int32)
    p2 = tok_t[1:2, :].astype(jnp.int32)
    rows = xs_ref.shape[0]
    ri = lax.broadcasted_iota(jnp.int32, (rows, tm), 0)
    perm = jnp.where((ri == p1) | (ri == p2), 1.0, 0.0).astype(BF16)
    xs_ref[...] = jnp.dot(perm, h2_hi, preferred_element_type=F32).astype(BF16)


def _out_route(x2, att, lru, wo, g, wr, br, tm):
    n, d = x2.shape
    aw = att.shape[1]
    nblk = n // tm
    rows = _block_rows(tm)
    blk = lambda i: jnp.minimum(i, nblk - 1)
    row = lambda c: pl.BlockSpec((tm, c), lambda i: (blk(i), 0))
    full = lambda shape: pl.BlockSpec(shape, lambda i: (0,) * len(shape))
    return pl.pallas_call(
        _out_route_kernel,
        out_shape=(jax.ShapeDtypeStruct((n, d), F32),
                   jax.ShapeDtypeStruct(((nblk + 1) * rows, d), BF16),
                   jax.ShapeDtypeStruct((n, LANES), F32),
                   jax.ShapeDtypeStruct((nblk, SUBLANES, LANES), F32)),
        grid=(nblk + 1,),
        in_specs=[row(d), row(aw), row(d - aw), full((d, d)), full((1, d)),
                  full((2, d, LANES)), full((1, LANES))],
        out_specs=(row(d), pl.BlockSpec((rows, d), lambda i: (i, 0)), row(LANES),
                   pl.BlockSpec((1, SUBLANES, LANES), lambda i: (blk(i), 0, 0))),
        compiler_params=pltpu.CompilerParams(
            dimension_semantics=("arbitrary",), vmem_limit_bytes=VMEM_LIMIT),
        name="out_route",
    )(x2, att, lru, wo, g, wr, br)


GATHER_AHEAD = 2
X_SLOTS = GATHER_AHEAD + 1
Y_SLOTS = 2


def _experts_kernel(te_ref, nu_ref, src_ref, dst_ref, wnext_ref, wslot_ref,
                    xs_hbm, wg_hbm, wu_hbm, wd_hbm,
                    ys_hbm, xbuf, ybuf, wgbuf, wubuf, wdbuf, wg16, wu16, wd16, gsem, ssem, wsem):
    del xs_hbm
    i = pl.program_id(0)
    nu = nu_ref[0]

    def gather(tile, k):
        row = pl.multiple_of(src_ref[tile * TILE_CHUNKS + k], CHUNK)
        slot = tile % X_SLOTS
        return pltpu.make_async_copy(ys_hbm.at[pl.ds(row, CHUNK), :],
                                     xbuf.at[slot, pl.ds(k * CHUNK, CHUNK), :], gsem.at[slot])

    def scatter(tile, k):
        row = pl.multiple_of(dst_ref[tile * TILE_CHUNKS + k], CHUNK)
        slot = tile % Y_SLOTS
        return pltpu.make_async_copy(ybuf.at[slot, pl.ds(k * CHUNK, CHUNK), :],
                                     ys_hbm.at[pl.ds(row, CHUNK), :], ssem.at[slot])

    def each_chunk(copy, tile, op):
        for k in range(TILE_CHUNKS):
            getattr(copy(tile, k), op)()

    def weight_copies(tile):
        e = te_ref[tile]
        s = wslot_ref[tile]
        return [pltpu.make_async_copy(hbm.at[e], buf.at[s], wsem.at[s])
                for hbm, buf in ((wg_hbm, wgbuf), (wu_hbm, wubuf), (wd_hbm, wdbuf))]

    @pl.when((i == 0) & (nu > 0))
    def _():
        for ahead in range(GATHER_AHEAD):
            each_chunk(gather, ahead, "start")
        for c in weight_copies(0):
            c.start()

    @pl.when((i < nu) & ((i == 0) | (te_ref[i] != te_ref[jnp.maximum(i - 1, 0)])))
    def _():
        for c in weight_copies(i):
            c.wait()
        s = wslot_ref[i]
        wg16[...] = wgbuf[s].astype(BF16)
        wu16[...] = wubuf[s].astype(BF16)
        wd16[...] = wdbuf[s].astype(BF16)
        nxt = wnext_ref[i]

        @pl.when(nxt < nu)
        def _():
            for c in weight_copies(nxt):
                c.start()

    @pl.when(i < nu)
    def _():
        each_chunk(gather, i, "wait")
        x = xbuf[i % X_SLOTS]
        g = jnp.dot(x, wg16[...], preferred_element_type=F32)
        u = jnp.dot(x, wu16[...], preferred_element_type=F32)
        h = (g * _sigmoid(g) * u).astype(BF16)
        ybuf[i % Y_SLOTS] = jnp.dot(h, wd16[...], preferred_element_type=F32).astype(BF16)
        each_chunk(scatter, i, "start")
        each_chunk(gather, i + GATHER_AHEAD, "start")

        @pl.when(i > 0)
        def _():
            each_chunk(scatter, i - 1, "wait")

        @pl.when(i == nu - 1)
        def _():
            for ahead in range(1, GATHER_AHEAD + 1):
                each_chunk(gather, i + ahead, "wait")
            each_chunk(scatter, i, "wait")


W_SLOTS = 2


def _experts(tile_expert, n_used, chunk_src, chunk_dst, w_next, w_slot, xs, wg, wu, wd):
    d = xs.shape[1]
    f = wg.shape[2]
    tm = TILE_CHUNKS * CHUNK
    n_tiles = tile_expert.shape[0]
    n_prefetch = 6
    grid_spec = pltpu.PrefetchScalarGridSpec(
        num_scalar_prefetch=n_prefetch,
        grid=(n_tiles,),
        in_specs=[pl.BlockSpec(memory_space=pl.ANY)] * 4,
        out_specs=pl.BlockSpec(memory_space=pl.ANY),
        scratch_shapes=[pltpu.VMEM((X_SLOTS, tm, d), BF16), pltpu.VMEM((Y_SLOTS, tm, d), BF16),
                        pltpu.VMEM((W_SLOTS, d, f), F32), pltpu.VMEM((W_SLOTS, d, f), F32),
                        pltpu.VMEM((W_SLOTS, f, d), F32),
                        pltpu.VMEM((d, f), BF16), pltpu.VMEM((d, f), BF16), pltpu.VMEM((f, d), BF16),
                        pltpu.SemaphoreType.DMA((X_SLOTS,)), pltpu.SemaphoreType.DMA((Y_SLOTS,)),
                        pltpu.SemaphoreType.DMA((W_SLOTS,))],
    )
    return pl.pallas_call(
        _experts_kernel,
        out_shape=jax.ShapeDtypeStruct(xs.shape, BF16),
        grid_spec=grid_spec,
        input_output_aliases={n_prefetch: 0},
        compiler_params=pltpu.CompilerParams(
            dimension_semantics=("arbitrary",), vmem_limit_bytes=VMEM_LIMIT),
        name="experts",
    )(tile_expert, n_used, chunk_src, chunk_dst, w_next, w_slot, xs, wg, wu, wd)


def _combine_kernel(x1_ref, tok_ref, g_ref, ys_ref, o_ref):
    tm = x1_ref.shape[0]
    rows = ys_ref.shape[0]
    tok = tok_ref[...]
    pos1 = tok[:, 0:1].astype(jnp.int32)
    pos2 = tok[:, 1:2].astype(jnp.int32)
    ci = lax.broadcasted_iota(jnp.int32, (tm, rows), 1)
    unperm = jnp.where(ci == pos1, tok[:, 2:3], jnp.where(ci == pos2, tok[:, 3:4], 0.0))
    y = jnp.dot(unperm.astype(BF16), ys_ref[...], preferred_element_type=F32)
    o_ref[...] = _rms(x1_ref[...] + y, g_ref[...])


def _combine(x1, tok, g, ys, tm):
    n, d = x1.shape
    rows = _block_rows(tm)
    return pl.pallas_call(
        _combine_kernel,
        out_shape=jax.ShapeDtypeStruct((n, d), F32),
        grid=(n // tm,),
        in_specs=[pl.BlockSpec((tm, d), lambda i: (i, 0)),
                  pl.BlockSpec((tm, LANES), lambda i: (i, 0)),
                  pl.BlockSpec((1, d), lambda i: (0, 0)),
                  pl.BlockSpec((rows, d), lambda i: (i, 0))],
        out_specs=pl.BlockSpec((tm, d), lambda i: (i, 0)),
        compiler_params=pltpu.CompilerParams(
            dimension_semantics=("arbitrary",), vmem_limit_bytes=VMEM_LIMIT),
        name="combine",
    )(x1, tok, g, ys)


def _block_diag(w):
    nb, bd, _ = w.shape
    eye = jnp.eye(nb, dtype=w.dtype)
    return (eye[:, None, :, None] * w[:, :, None, :]).reshape(nb * bd, nb * bd)


def _tile_tables(nch, rows):
    nblk = nch.shape[0]
    ex_cum = lambda a, axis: jnp.cumsum(a, axis=axis) - a
    ch_off = ex_cum(nch, 1)
    blk_cum = ex_cum(nch, 0)
    tot = jnp.sum(nch, axis=0)
    tiles_e = (tot + TILE_CHUNKS - 1) // TILE_CHUNKS
    t_end = jnp.cumsum(tiles_e)
    t_start = t_end - tiles_e
    max_chunks = nblk * ((rows - N_EXPERTS * CHUNK) // CHUNK + N_EXPERTS)
    n_tiles = pl.cdiv(max_chunks, TILE_CHUNKS) + N_EXPERTS
    tid = jnp.arange(n_tiles + GATHER_AHEAD, dtype=jnp.int32)
    slot = jnp.arange(TILE_CHUNKS, dtype=jnp.int32)[None, :]
    te = jnp.minimum(jnp.sum((tid[:, None] >= t_end[None, :]).astype(jnp.int32), axis=1),
                     N_EXPERTS - 1)
    is_te = te[:, None] == jnp.arange(N_EXPERTS, dtype=jnp.int32)[None, :]
    of_te = lambda v: jnp.sum(jnp.where(is_te[(...,) + (None,) * (v.ndim - 1)], v[None], 0), axis=1)
    q = (tid - of_te(t_start))[:, None] * TILE_CHUNKS + slot
    valid = (tid < t_end[-1])[:, None] & (q < of_te(tot)[:, None])
    seg_start = of_te(blk_cum.T)
    seg_end = seg_start + of_te(nch.T)
    bidx = jnp.minimum(jnp.sum((q[:, :, None] >= seg_end[:, None, :]).astype(jnp.int32), axis=-1),
                       nblk - 1)
    is_b = bidx[:, :, None] == jnp.arange(nblk, dtype=jnp.int32)[None, None, :]
    of_b = lambda v: jnp.sum(jnp.where(is_b, v[:, None, :], 0), axis=-1)
    row = bidx * rows + CHUNK * (of_b(of_te(ch_off.T)) + q - of_b(seg_start))
    src = jnp.where(valid, row, rows - CHUNK).astype(jnp.int32)
    assert Y_SLOTS * TILE_CHUNKS * CHUNK <= rows
    spare = nblk * rows + CHUNK * ((tid % Y_SLOTS)[:, None] * TILE_CHUNKS + slot)
    dst = jnp.where(valid, row, spare).astype(jnp.int32)
    w_next = of_te(t_end).astype(jnp.int32)
    w_slot = (of_te(jnp.cumsum((tiles_e > 0).astype(jnp.int32))) % W_SLOTS).astype(jnp.int32)
    return (te[:n_tiles], t_end[-1:].astype(jnp.int32), src.reshape(-1),
            dst[:n_tiles].reshape(-1), w_next[:n_tiles], w_slot[:n_tiles])


def kernel(x, mix_norm, w_in, b_forget, conv_w, conv_b, w_a, b_a, w_x, b_x, lru_lambda, w_out,
           ffn_norm, w_group, b_group, w_inner, b_inner, w_gate, w_up, w_down, final_norm):
    b, s, d = x.shape
    assert w_in.shape[0] == 1, "the final rmsnorm is fused into the single layer's combine stage"
    n = b * s
    tm = min(512, s)
    x2 = x.reshape(n, d)

    sizes = (ATT_WIDTH, ATT_WIDTH, ATT_WIDTH, ATT_HEADS, LRU_WIDTH, LRU_WIDTH)
    o = [0]
    for v in sizes:
        o.append(o[-1] + v)
    w_f = jnp.pad(w_in[0][:, o[3]:o[4]], ((0, 0), (0, LANES - ATT_HEADS)))
    w_cat = jnp.concatenate([w_in[0][:, :o[3]], w_in[0][:, o[4]:], w_f], axis=1).astype(BF16)
    bfg = jnp.pad(b_forget[0], (0, LANES - ATT_HEADS)).reshape(1, LANES)
    qkv, lru, cs = _mix_in(x, mix_norm.reshape(1, d), w_cat,
                           conv_w[0], conv_b.reshape(1, -1),
                           _block_diag(w_a[0]).astype(BF16), b_a.reshape(1, -1),
                           _block_diag(w_x[0]).astype(BF16), b_x.reshape(1, -1),
                           lru_lambda.reshape(1, -1), bfg, tm)

    att = _attn(qkv, cs, min(1024, s))

    wr = jnp.concatenate(
        [w_group[0], jnp.transpose(w_inner[0], (1, 0, 2)).reshape(d, N_EXPERTS)], axis=1)
    wr = jnp.pad(wr, ((0, 0), (0, LANES - wr.shape[1])))
    wr_hi = wr.astype(BF16)
    wr = jnp.stack([wr_hi, (wr - wr_hi.astype(F32)).astype(BF16)])
    br = jnp.concatenate([b_group[0], b_inner[0].reshape(-1)])
    br = jnp.pad(br, (0, LANES - br.shape[0])).reshape(1, LANES)
    x1, xs, tok, nch = _out_route(x2, att.reshape(n, -1), lru.reshape(n, -1),
                                  w_out[0].astype(BF16), ffn_norm.reshape(1, d), wr, br, tm)

    rows = _block_rows(tm)
    tables = _tile_tables(nch[:, 0, :N_EXPERTS].astype(jnp.int32), rows)
    fe = w_gate.shape[-1]
    ys = _experts(*tables, xs, w_gate.reshape(N_EXPERTS, d, fe), w_up.reshape(N_EXPERTS, d, fe),
                  w_down.reshape(N_EXPERTS, fe, d))
    out = _combine(x1, tok, final_norm.reshape(1, d), ys, tm)
    return out.reshape(b, s, d)
```

```python
import functools

import jax
import jax.numpy as jnp
from jax import lax
from jax.experimental import pallas as pl
from jax.experimental.pallas import tpu as pltpu

ATT_HEADS = 8
HEAD_DIM = 64
ATT_WIDTH = ATT_HEADS * HEAD_DIM
LRU_WIDTH = 512
LRU_BLOCKS = 8
CONV_WIDTH = 4
LRU_C = 8.0
N_GROUPS = 4
EXPERTS_PER_GROUP = 8
N_EXPERTS = N_GROUPS * EXPERTS_PER_GROUP
D_EXPERT = 256
RMS_EPS = 1e-6
LANES = 128
SUBLANES = 8
CHUNK = 16
TILE_CHUNKS = 16
MASK_VALUE = -1e30
LOG2E = 1.4426950408889634
DECAY_PARTS = 3
VMEM_LIMIT = 48 * 1024 * 1024

BF16 = jnp.bfloat16
F32 = jnp.float32


def _rms(x, g):
    inv = lax.rsqrt(jnp.mean(x * x, axis=-1, keepdims=True) + RMS_EPS)
    return x * inv * g


def _block_rows(tm):
    return 2 * tm + N_EXPERTS * CHUNK


def _sigmoid(v):
    return 0.5 * jnp.tanh(0.5 * v) + 0.5


def _scan_rows(a, u, carry):
    ts, c = u.shape
    g = ts // SUBLANES
    u3 = u.reshape(g, SUBLANES, c)
    a3 = None if a is None else a.reshape(g, SUBLANES, c)
    sub = lax.broadcasted_iota(jnp.int32, u3.shape, 1)
    d = 1
    while d < SUBLANES:
        keep = sub >= d
        prev_u = jnp.where(keep, pltpu.roll(u3, d, 1), 0.0)
        if a3 is None:
            u3 = u3 + prev_u
        else:
            u3 = a3 * prev_u + u3
            a3 = a3 * jnp.where(keep, pltpu.roll(a3, d, 1), 1.0)
        d *= 2
    h_in = jnp.broadcast_to(carry, (SUBLANES, c))
    out = []
    for k in range(g):
        hk = u3[k] + (h_in if a3 is None else a3[k] * h_in)
        out.append(hk)
        h_in = jnp.broadcast_to(hk[SUBLANES - 1:, :], (SUBLANES, c))
    return jnp.concatenate(out, axis=0)


def _mix_in_kernel(x_ref, g_ref, w_ref, cw_ref, cb_ref, wa_ref, ba_ref, wx_ref, bx_ref, lam_ref,
                   bf_ref, qkv_ref, lru_ref, cs_ref, xprev, hprev, cprev):
    @pl.when(pl.program_id(1) == 0)
    def _():
        xprev[...] = jnp.zeros_like(xprev)
        hprev[...] = jnp.zeros_like(hprev)
        cprev[...] = jnp.zeros_like(cprev)

    ts = x_ref.shape[1]
    w = LRU_WIDTH
    aw = ATT_WIDTH
    nq = 3 * aw
    hn = _rms(x_ref[0], g_ref[...]).astype(BF16)
    gate = jnp.dot(hn, w_ref[:, nq:nq + w], preferred_element_type=F32)
    rec = jnp.dot(hn, w_ref[:, nq + w:nq + 2 * w], preferred_element_type=F32)
    f_logit = jnp.dot(hn, w_ref[:, nq + 2 * w:], preferred_element_type=F32)
    q = jnp.dot(hn, w_ref[:, :aw], preferred_element_type=F32) * (HEAD_DIM ** -0.5 * LOG2E)
    qkv_ref[0, :, :aw] = q.astype(BF16)
    qkv_ref[0, :, aw:] = jnp.dot(hn, w_ref[:, aw:nq], preferred_element_type=F32).astype(BF16)
    row8 = lax.broadcasted_iota(jnp.int32, (SUBLANES, w), 0)

    prev = xprev[...]
    conv = cb_ref[...] + cw_ref[CONV_WIDTH - 1:CONV_WIDTH, :] * rec
    for k in range(1, CONV_WIDTH):
        rolled = pltpu.roll(rec, k, 0)
        head = jnp.where(row8 < k, pltpu.roll(prev, k, 0), rolled[:SUBLANES])
        shifted = jnp.concatenate([head, rolled[SUBLANES:]], axis=0)
        conv = conv + cw_ref[CONV_WIDTH - 1 - k:CONV_WIDTH - k, :] * shifted
    xprev[...] = rec[ts - SUBLANES:, :]

    cb16 = conv.astype(BF16)
    r = _sigmoid(jnp.dot(cb16, wa_ref[...], preferred_element_type=F32) + ba_ref[...])
    i = _sigmoid(jnp.dot(cb16, wx_ref[...], preferred_element_type=F32) + bx_ref[...])
    log_a = (-LRU_C) * r * jax.nn.softplus(-lam_ref[...])
    a = jnp.exp(log_a)
    th = jnp.tanh(log_a)
    one_minus_a2 = -2.0 * th / (1.0 - th)
    mult = jnp.where(one_minus_a2 > 0.0, one_minus_a2 * lax.rsqrt(one_minus_a2), 0.0)
    u = mult * (i * conv)

    h = _scan_rows(a, u, hprev[...])
    hprev[...] = h[ts - 1:, :]
    lru_ref[0] = (h * jax.nn.gelu(gate, approximate=True)).astype(BF16)

    c = _scan_rows(None, jax.nn.log_sigmoid(f_logit + bf_ref[...]), cprev[...])
    lanec = lax.broadcasted_iota(jnp.int32, (ts, LANES), 1)
    cprev[...] = c[ts - 1:, :]
    rest = c * (-LOG2E)
    parts = jnp.zeros_like(rest)
    for j in range(DECAY_PARTS):
        piece = rest.astype(BF16).astype(F32)
        rest = rest - piece
        shifted = piece if j == 0 else pltpu.roll(piece, ATT_HEADS * j, 1)
        parts = jnp.where((lanec >= ATT_HEADS * j) & (lanec < ATT_HEADS * (j + 1)), shifted, parts)
    cs_ref[0] = parts.astype(BF16)


def _mix_in(x, g, w_cat, cw, cb, wa, ba, wx, bx, lam, bfg, ts):
    b, s, d = x.shape
    w = LRU_WIDTH
    full = lambda shape: pl.BlockSpec(shape, lambda bi, ti: (0,) * len(shape))
    tile = lambda c: pl.BlockSpec((1, ts, c), lambda bi, ti: (bi, ti, 0))
    return pl.pallas_call(
        _mix_in_kernel,
        out_shape=(jax.ShapeDtypeStruct((b, s, 3 * ATT_WIDTH), BF16),
                   jax.ShapeDtypeStruct((b, s, w), BF16),
                   jax.ShapeDtypeStruct((b, s, LANES), BF16)),
        grid=(b, s // ts),
        in_specs=[tile(d), full((1, d)), full(w_cat.shape),
                  full((CONV_WIDTH, w)), full((1, w)),
                  full((w, w)), full((1, w)), full((w, w)), full((1, w)),
                  full((1, w)), full((1, LANES))],
        out_specs=(tile(3 * ATT_WIDTH), tile(w), tile(LANES)),
        scratch_shapes=[pltpu.VMEM((SUBLANES, w), F32),
                        pltpu.VMEM((1, w), F32),
                        pltpu.VMEM((1, LANES), F32)],
        compiler_params=pltpu.CompilerParams(
            dimension_semantics=("arbitrary", "arbitrary"), vmem_limit_bytes=VMEM_LIMIT),
        name="mix_in",
    )(x, g, w_cat, cw, cb, wa, ba, wx, bx, lam, bfg)


def _attn_kernel(q_ref, k_ref, v_ref, cs_ref, o_ref, ka_e, ka_o, va_e, va_o,
                 m_e, m_o, acc_e, acc_o, *, t):
    head0 = 2 * pl.program_id(1)
    i = pl.program_id(2)
    lane = lax.broadcasted_iota(jnp.int32, (1, LANES), 1)
    lo = lane < HEAD_DIM
    den_e = HEAD_DIM

    def ones_where(cond, rows):
        return jnp.broadcast_to(jnp.where(cond, 1.0, 0.0), (rows, LANES)).astype(BF16)

    @pl.when(i == 0)
    def _():
        r = lax.broadcasted_iota(jnp.int32, (LANES, LANES), 0)
        c = lax.broadcasted_iota(jnp.int32, (LANES, LANES), 1)

        def pieces(head, base):
            j = c - base
            sel = jnp.where((j >= 0) & (j < DECAY_PARTS) & (r == head + ATT_HEADS * j), 1.0, 0.0)
            return jnp.dot(cs_ref[0], sel.astype(BF16), preferred_element_type=F32).astype(BF16)

        k2 = k_ref[0]
        v2 = v_ref[0]
        ka_e[...] = jnp.where(lo, k2, pieces(head0, HEAD_DIM))
        ka_o[...] = jnp.where(lo, pieces(head0 + 1, 0), k2)
        va_e[...] = jnp.where(lo, v2, ones_where(lane == den_e, v2.shape[0]))
        va_o[...] = jnp.where(lo, ones_where(lane == 0, v2.shape[0]), v2)

    q = q_ref[0]
    qa_e = jnp.where(lo, q, ones_where(lane < HEAD_DIM + DECAY_PARTS, t))
    qa_o = jnp.where(lo, ones_where(lane < DECAY_PARTS, t), q)
    nt = (((1,), (1,)), ((), ()))

    heads = ((qa_e, ka_e, va_e, m_e, acc_e), (qa_o, ka_o, va_o, m_o, acc_o))
    for _, _, _, m_ref, acc_ref in heads:
        m_ref[...] = jnp.full(m_ref.shape, MASK_VALUE, F32)
        acc_ref[...] = jnp.zeros(acc_ref.shape, F32)

    def block(row0, nrows, key0, width, key_rel):
        key0 = pl.multiple_of(key0, t // 2)
        rows = slice(row0, row0 + nrows)
        if key_rel is not None:
            rr = lax.broadcasted_iota(jnp.int32, (nrows, width), 0) + row0
            cc = lax.broadcasted_iota(jnp.int32, (nrows, width), 1) + key_rel
            keep = cc <= rr
        updates = []
        for qa, ka, va, m_ref, acc_ref in heads:
            s = lax.dot_general(qa[rows], ka[pl.ds(key0, width), :], nt,
                                preferred_element_type=F32)
            if key_rel is not None:
                s = jnp.where(keep, s, MASK_VALUE)
            cols = [s[:, c * LANES:(c + 1) * LANES] for c in range(width // LANES)]
            m_old = m_ref[rows, :]
            row_max = jnp.max(functools.reduce(jnp.maximum, cols), axis=-1, keepdims=True)
            m_new = jnp.maximum(m_old, row_max)
            p = jnp.concatenate([jnp.exp2(c - m_new) for c in cols], axis=1).astype(BF16)
            acc = acc_ref[rows, :] * jnp.exp2(m_old - m_new) + jnp.dot(
                p, va[pl.ds(key0, width), :], preferred_element_type=F32)
            updates.append((m_ref, m_new, acc_ref, acc))
        for m_ref, m_new, acc_ref, acc in updates:
            m_ref[rows, :] = m_new
            acc_ref[rows, :] = acc

    def full(j, c):
        block(0, t, j * t, t, None)
        return c

    lax.fori_loop(0, i, full, 0)
    half = t // 2
    block(0, t, i * t, half, 0)
    block(half, half, i * t + half, half, half)

    a_e = acc_e[...]
    a_o = acc_o[...]
    o_ref[0] = jnp.where(lo, a_e / a_e[:, den_e:den_e + 1], a_o / a_o[:, 0:1]).astype(BF16)


def _attn(qkv, cs, t):
    b, s, _ = qkv.shape
    hp = ATT_WIDTH // LANES
    return pl.pallas_call(
        functools.partial(_attn_kernel, t=t),
        out_shape=jax.ShapeDtypeStruct((b, s, ATT_WIDTH), BF16),
        grid=(b, hp, s // t),
        in_specs=[pl.BlockSpec((1, t, LANES), lambda bi, h, i: (bi, i, h)),
                  pl.BlockSpec((1, s, LANES), lambda bi, h, i: (bi, 0, hp + h)),
                  pl.BlockSpec((1, s, LANES), lambda bi, h, i: (bi, 0, 2 * hp + h)),
                  pl.BlockSpec((1, s, LANES), lambda bi, h, i: (bi, 0, 0))],
        out_specs=pl.BlockSpec((1, t, LANES), lambda bi, h, i: (bi, i, h)),
        scratch_shapes=([pltpu.VMEM((s, LANES), BF16)] * 4 + [pltpu.VMEM((t, LANES), F32)] * 2
                        + [pltpu.VMEM((t, LANES), F32)] * 2),
        compiler_params=pltpu.CompilerParams(
            dimension_semantics=("arbitrary", "arbitrary", "arbitrary"),
            vmem_limit_bytes=VMEM_LIMIT),
        name="attn",
    )(qkv, qkv, qkv, cs)


def _out_route_kernel(*refs):
    xs_ref = refs[8]
    last = pl.num_programs(0) - 1
    pl.when(pl.program_id(0) < last)(functools.partial(_route_block, *refs))

    @pl.when(pl.program_id(0) == last)
    def _():
        xs_ref[...] = jnp.zeros_like(xs_ref)


def _route_block(x_ref, att_ref, lru_ref, wo_ref, g_ref, wr_ref, br_ref,
                 x1_ref, xs_ref, tok_ref, nch_ref):
    tm = x_ref.shape[0]
    aw = att_ref.shape[1]
    x1 = (x_ref[...]
          + jnp.dot(att_ref[...], wo_ref[:aw, :], preferred_element_type=F32)
          + jnp.dot(lru_ref[...], wo_ref[aw:, :], preferred_element_type=F32))
    x1_ref[...] = x1
    h2 = _rms(x1, g_ref[...])
    h2_hi = h2.astype(BF16)
    h2_lo = (h2 - h2_hi.astype(F32)).astype(BF16)

    logits = (jnp.dot(h2_hi, wr_ref[0], preferred_element_type=F32)
              + jnp.dot(h2_lo, wr_ref[0], preferred_element_type=F32)
              + jnp.dot(h2_hi, wr_ref[1], preferred_element_type=F32)) + br_ref[...]
    lane = lax.broadcasted_iota(jnp.int32, (tm, LANES), 1)
    gl = jnp.where(lane < N_GROUPS, logits, MASK_VALUE)
    gmax = jnp.max(gl, axis=-1, keepdims=True)
    gsum = jnp.sum(jnp.exp(gl - gmax), axis=-1, keepdims=True)
    g_w = 1.0 / gsum
    g_idx = jnp.min(jnp.where(gl == gmax, lane, LANES), axis=-1, keepdims=True)
    in_group = (lane >= N_GROUPS) & (lane < N_GROUPS + N_EXPERTS) & (
        lax.shift_right_logical(lane - N_GROUPS, 3) == g_idx)
    il = jnp.where(in_group, logits, MASK_VALUE)
    m1 = jnp.max(il, axis=-1, keepdims=True)
    e1 = jnp.min(jnp.where(il == m1, lane, LANES), axis=-1, keepdims=True)
    il2 = jnp.where(lane == e1, MASK_VALUE, il)
    m2 = jnp.max(il2, axis=-1, keepdims=True)
    e2 = jnp.min(jnp.where(il2 == m2, lane, LANES), axis=-1, keepdims=True)
    ratio = jnp.exp(m2 - m1)
    w1 = g_w / (1.0 + ratio)
    w2 = w1 * ratio
    hot1 = lane == e1 - N_GROUPS
    hot2 = lane == e2 - N_GROUPS

    onehot = jnp.where(hot1 | hot2, 1.0, 0.0)
    rr = lax.broadcasted_iota(jnp.int32, (tm, tm), 0)
    cc = lax.broadcasted_iota(jnp.int32, (tm, tm), 1)
    tri = jnp.where(cc < rr, 1.0, 0.0).astype(BF16)
    before = jnp.dot(tri, onehot.astype(BF16), preferred_element_type=F32)
    cnt = jnp.sum(onehot, axis=0, keepdims=True)
    nch = jnp.floor((cnt + (CHUNK - 1)) * (1.0 / CHUNK))
    r128 = lax.broadcasted_iota(jnp.int32, (LANES, LANES), 0)
    c128 = lax.broadcasted_iota(jnp.int32, (LANES, LANES), 1)
    upper = jnp.where(r128 < c128, 1.0, 0.0).astype(BF16)
    nch8 = jnp.broadcast_to(nch, (SUBLANES, LANES))
    seg_off = CHUNK * jnp.dot(nch8.astype(BF16), upper, preferred_element_type=F32)[0:1, :]
    pos = seg_off + before
    pos1 = jnp.sum(jnp.where(hot1, pos, 0.0), axis=-1, keepdims=True)
    pos2 = jnp.sum(jnp.where(hot2, pos, 0.0), axis=-1, keepdims=True)
    nch_ref[0] = nch8

    tok = jnp.where(lane == 0, pos1, 0.0)
    tok = jnp.where(lane == 1, pos2, tok)
    tok = jnp.where(lane == 2, w1, tok)
    tok = jnp.where(lane == 3, w2, tok)
    tok_ref[...] = tok

    tok_t = jnp.transpose(tok)
    p1 = tok_t[0:1, :].astype(jnp.int32)
    p2 = tok_t[1:2, :].astype(jnp.int32)
    rows = xs_ref.shape[0]
    ri = lax.broadcasted_iota(jnp.int32, (rows, tm), 0)
    perm = jnp.where((ri == p1) | (ri == p2), 1.0, 0.0).astype(BF16)
    xs_ref[...] = jnp.dot(perm, h2_hi, preferred_element_type=F32).astype(BF16)


def _out_route(x2, att, lru, wo, g, wr, br, tm):
    n, d = x2.shape
    aw = att.shape[1]
    nblk = n // tm
    rows = _block_rows(tm)
    blk = lambda i: jnp.minimum(i, nblk - 1)
    row = lambda c: pl.BlockSpec((tm, c), lambda i: (blk(i), 0))
    full = lambda shape: pl.BlockSpec(shape, lambda i: (0,) * len(shape))
    return pl.pallas_call(
        _out_route_kernel,
        out_shape=(jax.ShapeDtypeStruct((n, d), F32),
                   jax.ShapeDtypeStruct(((nblk + 1) * rows, d), BF16),
                   jax.ShapeDtypeStruct((n, LANES), F32),
                   jax.ShapeDtypeStruct((nblk, SUBLANES, LANES), F32)),
        grid=(nblk + 1,),
        in_specs=[row(d), row(aw), row(d - aw), full((d, d)), full((1, d)),
                  full((2, d, LANES)), full((1, LANES))],
        out_specs=(row(d), pl.BlockSpec((rows, d), lambda i: (i, 0)), row(LANES),
                   pl.BlockSpec((1, SUBLANES, LANES), lambda i: (blk(i), 0, 0))),
        compiler_params=pltpu.CompilerParams(
            dimension_semantics=("arbitrary",), vmem_limit_bytes=VMEM_LIMIT),
        name="out_route",
    )(x2, att, lru, wo, g, wr, br)


GATHER_AHEAD = 2
X_SLOTS = GATHER_AHEAD + 1
Y_SLOTS = 2


def _experts_kernel(te_ref, nu_ref, src_ref, dst_ref, wnext_ref, wslot_ref,
                    xs_hbm, wg_hbm, wu_hbm, wd_hbm,
                    ys_hbm, xbuf, ybuf, wgbuf, wubuf, wdbuf, wg16, wu16, wd16, gsem, ssem, wsem):
    del xs_hbm
    i = pl.program_id(0)
    nu = nu_ref[0]

    def gather(tile, k):
        row = pl.multiple_of(src_ref[tile * TILE_CHUNKS + k], CHUNK)
        slot = tile % X_SLOTS
        return pltpu.make_async_copy(ys_hbm.at[pl.ds(row, CHUNK), :],
                                     xbuf.at[slot, pl.ds(k * CHUNK, CHUNK), :], gsem.at[slot])

    def scatter(tile, k):
        row = pl.multiple_of(dst_ref[tile * TILE_CHUNKS + k], CHUNK)
        slot = tile % Y_SLOTS
        return pltpu.make_async_copy(ybuf.at[slot, pl.ds(k * CHUNK, CHUNK), :],
                                     ys_hbm.at[pl.ds(row, CHUNK), :], ssem.at[slot])

    def each_chunk(copy, tile, op):
        kwargs = dict(priority=1) if (op == "start" and copy is scatter) else {}
        for k in range(TILE_CHUNKS):
            getattr(copy(tile, k), op)(**kwargs)

    def weight_copies(tile):
        e = te_ref[tile]
        s = wslot_ref[tile]
        return [pltpu.make_async_copy(hbm.at[e], buf.at[s], wsem.at[s])
                for hbm, buf in ((wg_hbm, wgbuf), (wu_hbm, wubuf), (wd_hbm, wdbuf))]

    @pl.when((i == 0) & (nu > 0))
    def _():
        for ahead in range(GATHER_AHEAD):
            each_chunk(gather, ahead, "start")
        for c in weight_copies(0):
            c.start()

    @pl.when((i < nu) & ((i == 0) | (te_ref[i] != te_ref[jnp.maximum(i - 1, 0)])))
    def _():
        for c in weight_copies(i):
            c.wait()
        s = wslot_ref[i]
        wg16[...] = wgbuf[s].astype(BF16)
        wu16[...] = wubuf[s].astype(BF16)
        wd16[...] = wdbuf[s].astype(BF16)
        nxt = wnext_ref[i]

        @pl.when(nxt < nu)
        def _():
            for c in weight_copies(nxt):
                c.start()

    @pl.when(i < nu)
    def _():
        each_chunk(gather, i, "wait")
        x = xbuf[i % X_SLOTS]
        g = jnp.dot(x, wg16[...], preferred_element_type=F32)
        u = jnp.dot(x, wu16[...], preferred_element_type=F32)
        h = (g * _sigmoid(g) * u).astype(BF16)
        ybuf[i % Y_SLOTS] = jnp.dot(h, wd16[...], preferred_element_type=F32).astype(BF16)
        each_chunk(scatter, i, "start")
        each_chunk(gather, i + GATHER_AHEAD, "start")

        @pl.when(i > 0)
        def _():
            each_chunk(scatter, i - 1, "wait")

        @pl.when(i == nu - 1)
        def _():
            for ahead in range(1, GATHER_AHEAD + 1):
                each_chunk(gather, i + ahead, "wait")
            each_chunk(scatter, i, "wait")


W_SLOTS = 2


def _experts(tile_expert, n_used, chunk_src, chunk_dst, w_next, w_slot, xs, wg, wu, wd):
    d = xs.shape[1]
    f = wg.shape[2]
    tm = TILE_CHUNKS * CHUNK
    n_tiles = tile_expert.shape[0]
    n_prefetch = 6
    grid_spec = pltpu.PrefetchScalarGridSpec(
        num_scalar_prefetch=n_prefetch,
        grid=(n_tiles,),
        in_specs=[pl.BlockSpec(memory_space=pl.ANY)] * 4,
        out_specs=pl.BlockSpec(memory_space=pl.ANY),
        scratch_shapes=[pltpu.VMEM((X_SLOTS, tm, d), BF16), pltpu.VMEM((Y_SLOTS, tm, d), BF16),
                        pltpu.VMEM((W_SLOTS, d, f), F32), pltpu.VMEM((W_SLOTS, d, f), F32),
                        pltpu.VMEM((W_SLOTS, f, d), F32),
                        pltpu.VMEM((d, f), BF16), pltpu.VMEM((d, f), BF16), pltpu.VMEM((f, d), BF16),
                        pltpu.SemaphoreType.DMA((X_SLOTS,)), pltpu.SemaphoreType.DMA((Y_SLOTS,)),
                        pltpu.SemaphoreType.DMA((W_SLOTS,))],
    )
    return pl.pallas_call(
        _experts_kernel,
        out_shape=jax.ShapeDtypeStruct(xs.shape, BF16),
        grid_spec=grid_spec,
        input_output_aliases={n_prefetch: 0},
        compiler_params=pltpu.CompilerParams(
            dimension_semantics=("arbitrary",), vmem_limit_bytes=VMEM_LIMIT),
        name="experts",
    )(tile_expert, n_used, chunk_src, chunk_dst, w_next, w_slot, xs, wg, wu, wd)


def _combine_kernel(x1_ref, tok_ref, g_ref, ys_ref, o_ref):
    tm = x1_ref.shape[0]
    rows = ys_ref.shape[0]
    tok = tok_ref[...]
    pos1 = tok[:, 0:1].astype(jnp.int32)
    pos2 = tok[:, 1:2].astype(jnp.int32)
    ci = lax.broadcasted_iota(jnp.int32, (tm, rows), 1)
    unperm = jnp.where(ci == pos1, tok[:, 2:3], jnp.where(ci == pos2, tok[:, 3:4], 0.0))
    y = jnp.dot(unperm.astype(BF16), ys_ref[...], preferred_element_type=F32)
    o_ref[...] = _rms(x1_ref[...] + y, g_ref[...])


def _combine(x1, tok, g, ys, tm):
    n, d = x1.shape
    rows = _block_rows(tm)
    return pl.pallas_call(
        _combine_kernel,
        out_shape=jax.ShapeDtypeStruct((n, d), F32),
        grid=(n // tm,),
        in_specs=[pl.BlockSpec((tm, d), lambda i: (i, 0)),
                  pl.BlockSpec((tm, LANES), lambda i: (i, 0)),
                  pl.BlockSpec((1, d), lambda i: (0, 0)),
                  pl.BlockSpec((rows, d), lambda i: (i, 0))],
        out_specs=pl.BlockSpec((tm, d), lambda i: (i, 0)),
        compiler_params=pltpu.CompilerParams(
            dimension_semantics=("arbitrary",), vmem_limit_bytes=VMEM_LIMIT),
        name="combine",
    )(x1, tok, g, ys)


def _block_diag(w):
    nb, bd, _ = w.shape
    eye = jnp.eye(nb, dtype=w.dtype)
    return (eye[:, None, :, None] * w[:, :, None, :]).reshape(nb * bd, nb * bd)


def _tile_tables(nch, rows):
    nblk = nch.shape[0]
    ex_cum = lambda a, axis: jnp.cumsum(a, axis=axis) - a
    ch_off = ex_cum(nch, 1)
    blk_cum = ex_cum(nch, 0)
    tot = jnp.sum(nch, axis=0)
    tiles_e = (tot + TILE_CHUNKS - 1) // TILE_CHUNKS
    t_end = jnp.cumsum(tiles_e)
    t_start = t_end - tiles_e
    max_chunks = nblk * ((rows - N_EXPERTS * CHUNK) // CHUNK + N_EXPERTS)
    n_tiles = pl.cdiv(max_chunks, TILE_CHUNKS) + N_EXPERTS
    tid = jnp.arange(n_tiles + GATHER_AHEAD, dtype=jnp.int32)
    slot = jnp.arange(TILE_CHUNKS, dtype=jnp.int32)[None, :]
    te = jnp.minimum(jnp.sum((tid[:, None] >= t_end[None, :]).astype(jnp.int32), axis=1),
                     N_EXPERTS - 1)
    is_te = te[:, None] == jnp.arange(N_EXPERTS, dtype=jnp.int32)[None, :]
    of_te = lambda v: jnp.sum(jnp.where(is_te[(...,) + (None,) * (v.ndim - 1)], v[None], 0), axis=1)
    q = (tid - of_te(t_start))[:, None] * TILE_CHUNKS + slot
    valid = (tid < t_end[-1])[:, None] & (q < of_te(tot)[:, None])
    seg_start = of_te(blk_cum.T)
    seg_end = seg_start + of_te(nch.T)
    bidx = jnp.minimum(jnp.sum((q[:, :, None] >= seg_end[:, None, :]).astype(jnp.int32), axis=-1),
                       nblk - 1)
    is_b = bidx[:, :, None] == jnp.arange(nblk, dtype=jnp.int32)[None, None, :]
    of_b = lambda v: jnp.sum(jnp.where(is_b, v[:, None, :], 0), axis=-1)
    row = bidx * rows + CHUNK * (of_b(of_te(ch_off.T)) + q - of_b(seg_start))
    src = jnp.where(valid, row, rows - CHUNK).astype(jnp.int32)
    assert Y_SLOTS * TILE_CHUNKS * CHUNK <= rows
    spare = nblk * rows + CHUNK * ((tid % Y_SLOTS)[:, None] * TILE_CHUNKS + slot)
    dst = jnp.where(valid, row, spare).astype(jnp.int32)
    w_next = of_te(t_end).astype(jnp.int32)
    w_slot = (of_te(jnp.cumsum((tiles_e > 0).astype(jnp.int32))) % W_SLOTS).astype(jnp.int32)
    return (te[:n_tiles], t_end[-1:].astype(jnp.int32), src.reshape(-1),
            dst[:n_tiles].reshape(-1), w_next[:n_tiles], w_slot[:n_tiles])


def kernel(x, mix_norm, w_in, b_forget, conv_w, conv_b, w_a, b_a, w_x, b_x, lru_lambda, w_out,
           ffn_norm, w_group, b_group, w_inner, b_inner, w_gate, w_up, w_down, final_norm):
    b, s, d = x.shape
    assert w_in.shape[0] == 1, "the final rmsnorm is fused into the single layer's combine stage"
    n = b * s
    tm = min(512, s)
    x2 = x.reshape(n, d)

    sizes = (ATT_WIDTH, ATT_WIDTH, ATT_WIDTH, ATT_HEADS, LRU_WIDTH, LRU_WIDTH)
    o = [0]
    for v in sizes:
        o.append(o[-1] + v)
    w_f = jnp.pad(w_in[0][:, o[3]:o[4]], ((0, 0), (0, LANES - ATT_HEADS)))
    w_cat = jnp.concatenate([w_in[0][:, :o[3]], w_in[0][:, o[4]:], w_f], axis=1).astype(BF16)
    bfg = jnp.pad(b_forget[0], (0, LANES - ATT_HEADS)).reshape(1, LANES)
    qkv, lru, cs = _mix_in(x, mix_norm.reshape(1, d), w_cat,
                           conv_w[0], conv_b.reshape(1, -1),
                           _block_diag(w_a[0]).astype(BF16), b_a.reshape(1, -1),
                           _block_diag(w_x[0]).astype(BF16), b_x.reshape(1, -1),
                           lru_lambda.reshape(1, -1), bfg, tm)

    att = _attn(qkv, cs, min(1024, s))

    wr = jnp.concatenate(
        [w_group[0], jnp.transpose(w_inner[0], (1, 0, 2)).reshape(d, N_EXPERTS)], axis=1)
    wr = jnp.pad(wr, ((0, 0), (0, LANES - wr.shape[1])))
    wr_hi = wr.astype(BF16)
    wr = jnp.stack([wr_hi, (wr - wr_hi.astype(F32)).astype(BF16)])
    br = jnp.concatenate([b_group[0], b_inner[0].reshape(-1)])
    br = jnp.pad(br, (0, LANES - br.shape[0])).reshape(1, LANES)
    x1, xs, tok, nch = _out_route(x2, att.reshape(n, -1), lru.reshape(n, -1),
                                  w_out[0].astype(BF16), ffn_norm.reshape(1, d), wr, br, tm)

    rows = _block_rows(tm)
    tables = _tile_tables(nch[:, 0, :N_EXPERTS].astype(jnp.int32), rows)
    fe = w_gate.shape[-1]
    ys = _experts(*tables, xs, w_gate.reshape(N_EXPERTS, d, fe), w_up.reshape(N_EXPERTS, d, fe),
                  w_down.reshape(N_EXPERTS, fe, d))
    out = _combine(x1, tok, final_norm.reshape(1, d), ys, tm)
    return out.reshape(b, s, d)
```

```python
import functools

import jax
import jax.numpy as jnp
from jax import lax
from jax.experimental import pallas as pl
from jax.experimental.pallas import tpu as pltpu

ATT_HEADS = 8
HEAD_DIM = 64
ATT_WIDTH = ATT_HEADS * HEAD_DIM
LRU_WIDTH = 512
LRU_BLOCKS = 8
CONV_WIDTH = 4
LRU_C = 8.0
N_GROUPS = 4
EXPERTS_PER_GROUP = 8
N_EXPERTS = N_GROUPS * EXPERTS_PER_GROUP
D_EXPERT = 256
RMS_EPS = 1e-6
LANES = 128
SUBLANES = 8
CHUNK = 16
TILE_CHUNKS = 16
MASK_VALUE = -1e30
LOG2E = 1.4426950408889634
DECAY_PARTS = 3
VMEM_LIMIT = 48 * 1024 * 1024

BF16 = jnp.bfloat16
F32 = jnp.float32


def _rms(x, g):
    inv = lax.rsqrt(jnp.mean(x * x, axis=-1, keepdims=True) + RMS_EPS)
    return x * inv * g


def _block_rows(tm):
    return 2 * tm + N_EXPERTS * CHUNK


def _sigmoid(v):
    return 0.5 * jnp.tanh(0.5 * v) + 0.5


def _scan_rows(a, u, carry):
    ts, c = u.shape
    g = ts // SUBLANES
    u3 = u.reshape(g, SUBLANES, c)
    a3 = None if a is None else a.reshape(g, SUBLANES, c)
    sub = lax.broadcasted_iota(jnp.int32, u3.shape, 1)
    d = 1
    while d < SUBLANES:
        keep = sub >= d
        prev_u = jnp.where(keep, pltpu.roll(u3, d, 1), 0.0)
        if a3 is None:
            u3 = u3 + prev_u
        else:
            u3 = a3 * prev_u + u3
            a3 = a3 * jnp.where(keep, pltpu.roll(a3, d, 1), 1.0)
        d *= 2
    h_in = jnp.broadcast_to(carry, (SUBLANES, c))
    out = []
    for k in range(g):
        hk = u3[k] + (h_in if a3 is None else a3[k] * h_in)
        out.append(hk)
        h_in = jnp.broadcast_to(hk[SUBLANES - 1:, :], (SUBLANES, c))
    return jnp.concatenate(out, axis=0)


def _mix_in_kernel(x_ref, g_ref, w_ref, cw_ref, cb_ref, wa_ref, ba_ref, wx_ref, bx_ref, lam_ref,
                   bf_ref, qkv_ref, lru_ref, cs_ref, xprev, hprev, cprev):
    @pl.when(pl.program_id(1) == 0)
    def _():
        xprev[...] = jnp.zeros_like(xprev)
        hprev[...] = jnp.zeros_like(hprev)
        cprev[...] = jnp.zeros_like(cprev)

    ts = x_ref.shape[1]
    w = LRU_WIDTH
    aw = ATT_WIDTH
    nq = 3 * aw
    hn = _rms(x_ref[0], g_ref[...]).astype(BF16)
    gate = jnp.dot(hn, w_ref[:, nq:nq + w], preferred_element_type=F32)
    rec = jnp.dot(hn, w_ref[:, nq + w:nq + 2 * w], preferred_element_type=F32)
    f_logit = jnp.dot(hn, w_ref[:, nq + 2 * w:], preferred_element_type=F32)
    q = jnp.dot(hn, w_ref[:, :aw], preferred_element_type=F32) * (HEAD_DIM ** -0.5 * LOG2E)
    qkv_ref[0, :, :aw] = q.astype(BF16)
    qkv_ref[0, :, aw:] = jnp.dot(hn, w_ref[:, aw:nq], preferred_element_type=F32).astype(BF16)
    row8 = lax.broadcasted_iota(jnp.int32, (SUBLANES, w), 0)

    prev = xprev[...]
    conv = cb_ref[...] + cw_ref[CONV_WIDTH - 1:CONV_WIDTH, :] * rec
    for k in range(1, CONV_WIDTH):
        rolled = pltpu.roll(rec, k, 0)
        head = jnp.where(row8 < k, pltpu.roll(prev, k, 0), rolled[:SUBLANES])
        shifted = jnp.concatenate([head, rolled[SUBLANES:]], axis=0)
        conv = conv + cw_ref[CONV_WIDTH - 1 - k:CONV_WIDTH - k, :] * shifted
    xprev[...] = rec[ts - SUBLANES:, :]

    cb16 = conv.astype(BF16)
    r = _sigmoid(jnp.dot(cb16, wa_ref[...], preferred_element_type=F32) + ba_ref[...])
    i = _sigmoid(jnp.dot(cb16, wx_ref[...], preferred_element_type=F32) + bx_ref[...])
    log_a = (-LRU_C) * r * jax.nn.softplus(-lam_ref[...])
    a = jnp.exp(log_a)
    th = jnp.tanh(log_a)
    one_minus_a2 = -2.0 * th / (1.0 - th)
    mult = jnp.where(one_minus_a2 > 0.0, one_minus_a2 * lax.rsqrt(one_minus_a2), 0.0)
    u = mult * (i * conv)

    h = _scan_rows(a, u, hprev[...])
    hprev[...] = h[ts - 1:, :]
    lru_ref[0] = (h * jax.nn.gelu(gate, approximate=True)).astype(BF16)

    c = _scan_rows(None, jax.nn.log_sigmoid(f_logit + bf_ref[...]), cprev[...])
    lanec = lax.broadcasted_iota(jnp.int32, (ts, LANES), 1)
    cprev[...] = c[ts - 1:, :]
    rest = c * (-LOG2E)
    parts = jnp.zeros_like(rest)
    for j in range(DECAY_PARTS):
        piece = rest.astype(BF16).astype(F32)
        rest = rest - piece
        shifted = piece if j == 0 else pltpu.roll(piece, ATT_HEADS * j, 1)
        parts = jnp.where((lanec >= ATT_HEADS * j) & (lanec < ATT_HEADS * (j + 1)), shifted, parts)
    cs_ref[0] = parts.astype(BF16)


def _mix_in(x, g, w_cat, cw, cb, wa, ba, wx, bx, lam, bfg, ts):
    b, s, d = x.shape
    w = LRU_WIDTH
    full = lambda shape: pl.BlockSpec(shape, lambda bi, ti: (0,) * len(shape))
    tile = lambda c: pl.BlockSpec((1, ts, c), lambda bi, ti: (bi, ti, 0))
    return pl.pallas_call(
        _mix_in_kernel,
        out_shape=(jax.ShapeDtypeStruct((b, s, 3 * ATT_WIDTH), BF16),
                   jax.ShapeDtypeStruct((b, s, w), BF16),
                   jax.ShapeDtypeStruct((b, s, LANES), BF16)),
        grid=(b, s // ts),
        in_specs=[tile(d), full((1, d)), full(w_cat.shape),
                  full((CONV_WIDTH, w)), full((1, w)),
                  full((w, w)), full((1, w)), full((w, w)), full((1, w)),
                  full((1, w)), full((1, LANES))],
        out_specs=(tile(3 * ATT_WIDTH), tile(w), tile(LANES)),
        scratch_shapes=[pltpu.VMEM((SUBLANES, w), F32),
                        pltpu.VMEM((1, w), F32),
                        pltpu.VMEM((1, LANES), F32)],
        compiler_params=pltpu.CompilerParams(
            dimension_semantics=("arbitrary", "arbitrary"), vmem_limit_bytes=VMEM_LIMIT),
        name="mix_in",
    )(x, g, w_cat, cw, cb, wa, ba, wx, bx, lam, bfg)


ATT_QUERY_TILE = 2048
ATT_KEY_STEP = 1024
ATT_DIAG_STRIP = 512


def _attn_kernel(q_ref, k_ref, v_ref, cs_ref, o_ref, ka_e, ka_o, va_e, va_o,
                 m_e, m_o, acc_e, acc_o, *, t):
    head0 = 2 * pl.program_id(1)
    i = pl.program_id(2)
    lane = lax.broadcasted_iota(jnp.int32, (1, LANES), 1)
    lo = lane < HEAD_DIM
    den_e = HEAD_DIM

    def ones_where(cond, rows):
        return jnp.broadcast_to(jnp.where(cond, 1.0, 0.0), (rows, LANES)).astype(BF16)

    @pl.when(i == 0)
    def _():
        r = lax.broadcasted_iota(jnp.int32, (LANES, LANES), 0)
        c = lax.broadcasted_iota(jnp.int32, (LANES, LANES), 1)
        j_e = c - HEAD_DIM
        pick_e = (j_e >= 0) & (j_e < DECAY_PARTS) & (r == head0 + ATT_HEADS * j_e)
        pick_o = (c < DECAY_PARTS) & (r == head0 + 1 + ATT_HEADS * c)
        sel = jnp.where(pick_e, 1.0, jnp.where(pick_o, 1.0, 0.0)).astype(BF16)
        pieces = jnp.dot(cs_ref[0], sel, preferred_element_type=F32).astype(BF16)

        k2 = k_ref[0]
        v2 = v_ref[0]
        ka_e[...] = jnp.where(lo, k2, pieces)
        ka_o[...] = jnp.where(lo, pieces, k2)
        va_e[...] = jnp.where(lo, v2, ones_where(lane == den_e, v2.shape[0]))
        va_o[...] = jnp.where(lo, ones_where(lane == 0, v2.shape[0]), v2)

    q = q_ref[0]
    qa_e = jnp.where(lo, q, ones_where(lane < HEAD_DIM + DECAY_PARTS, t))
    qa_o = jnp.where(lo, ones_where(lane < DECAY_PARTS, t), q)
    nt = (((1,), (1,)), ((), ()))

    heads = ((qa_e, ka_e, va_e, m_e, acc_e), (qa_o, ka_o, va_o, m_o, acc_o))
    for _, _, _, m_ref, acc_ref in heads:
        m_ref[...] = jnp.full(m_ref.shape, MASK_VALUE, F32)
        acc_ref[...] = jnp.zeros(acc_ref.shape, F32)

    def block(row0, nrows, key0, width, key_rel):
        key0 = pl.multiple_of(key0, width)
        rows = slice(row0, row0 + nrows)
        if key_rel is not None:
            rr = lax.broadcasted_iota(jnp.int32, (nrows, width), 0) + row0
            cc = lax.broadcasted_iota(jnp.int32, (nrows, width), 1) + key_rel
            keep = cc <= rr
        updates = []
        for qa, ka, va, m_ref, acc_ref in heads:
            s = lax.dot_general(qa[rows], ka[pl.ds(key0, width), :], nt,
                                preferred_element_type=F32)
            if key_rel is not None:
                s = jnp.where(keep, s, MASK_VALUE)
            cols = [s[:, c * LANES:(c + 1) * LANES] for c in range(width // LANES)]
            m_old = m_ref[rows, :]
            row_max = jnp.max(functools.reduce(jnp.maximum, cols), axis=-1, keepdims=True)
            m_new = jnp.maximum(m_old, row_max)
            p = jnp.concatenate([jnp.exp2(c - m_new) for c in cols], axis=1).astype(BF16)
            acc = acc_ref[rows, :] * jnp.exp2(m_old - m_new) + jnp.dot(
                p, va[pl.ds(key0, width), :], preferred_element_type=F32)
            updates.append((m_ref, m_new, acc_ref, acc))
        for m_ref, m_new, acc_ref, acc in updates:
            m_ref[rows, :] = m_new
            acc_ref[rows, :] = acc

    kstep = min(t, ATT_KEY_STEP)

    def full(j, c):
        block(0, t, j * kstep, kstep, None)
        return c

    lax.fori_loop(0, i * (t // kstep), full, 0)
    strip = min(t, ATT_DIAG_STRIP)
    for c in range(t // strip):
        block(c * strip, t - c * strip, i * t + c * strip, strip, c * strip)

    a_e = acc_e[...]
    a_o = acc_o[...]
    o_ref[0] = jnp.where(lo, a_e / a_e[:, den_e:den_e + 1], a_o / a_o[:, 0:1]).astype(BF16)


def _attn(qkv, cs, t):
    b, s, _ = qkv.shape
    hp = ATT_WIDTH // LANES
    return pl.pallas_call(
        functools.partial(_attn_kernel, t=t),
        out_shape=jax.ShapeDtypeStruct((b, s, ATT_WIDTH), BF16),
        grid=(b, hp, s // t),
        in_specs=[pl.BlockSpec((1, t, LANES), lambda bi, h, i: (bi, i, h)),
                  pl.BlockSpec((1, s, LANES), lambda bi, h, i: (bi, 0, hp + h)),
                  pl.BlockSpec((1, s, LANES), lambda bi, h, i: (bi, 0, 2 * hp + h)),
                  pl.BlockSpec((1, s, LANES), lambda bi, h, i: (bi, 0, 0))],
        out_specs=pl.BlockSpec((1, t, LANES), lambda bi, h, i: (bi, i, h)),
        scratch_shapes=([pltpu.VMEM((s, LANES), BF16)] * 4 + [pltpu.VMEM((t, LANES), F32)] * 2
                        + [pltpu.VMEM((t, LANES), F32)] * 2),
        compiler_params=pltpu.CompilerParams(
            dimension_semantics=("arbitrary", "arbitrary", "arbitrary"),
            vmem_limit_bytes=VMEM_LIMIT),
        name="attn",
    )(qkv, qkv, qkv, cs)


def _out_route_kernel(*refs):
    xs_ref = refs[8]
    last = pl.num_programs(0) - 1
    pl.when(pl.program_id(0) < last)(functools.partial(_route_block, *refs))

    @pl.when(pl.program_id(0) == last)
    def _():
        xs_ref[...] = jnp.zeros_like(xs_ref)


def _route_block(x_ref, att_ref, lru_ref, wo_ref, g_ref, wr_ref, br_ref,
                 x1_ref, xs_ref, tok_ref, nch_ref):
    tm = x_ref.shape[0]
    aw = att_ref.shape[1]
    x1 = (x_ref[...]
          + jnp.dot(att_ref[...], wo_ref[:aw, :], preferred_element_type=F32)
          + jnp.dot(lru_ref[...], wo_ref[aw:, :], preferred_element_type=F32))
    x1_ref[...] = x1
    h2 = _rms(x1, g_ref[...])
    h2_hi = h2.astype(BF16)
    h2_lo = (h2 - h2_hi.astype(F32)).astype(BF16)

    logits = (jnp.dot(h2_hi, wr_ref[0], preferred_element_type=F32)
              + jnp.dot(h2_lo, wr_ref[0], preferred_element_type=F32)
              + jnp.dot(h2_hi, wr_ref[1], preferred_element_type=F32)) + br_ref[...]
    lt = jnp.transpose(logits)
    sub = lax.broadcasted_iota(jnp.int32, (SUBLANES, tm), 0)
    gl = jnp.where(sub < N_GROUPS, lt[:SUBLANES], MASK_VALUE)
    gmax = jnp.max(gl, axis=0, keepdims=True)
    g_w = 1.0 / jnp.sum(jnp.exp(gl - gmax), axis=0, keepdims=True)
    g_idx = jnp.min(jnp.where(gl == gmax, sub, SUBLANES), axis=0, keepdims=True)
    il = lt[SUBLANES:2 * SUBLANES]
    for g in range(1, N_GROUPS):
        il = jnp.where(g_idx == g, lt[(g + 1) * SUBLANES:(g + 2) * SUBLANES], il)
    m1 = jnp.max(il, axis=0, keepdims=True)
    e1 = jnp.min(jnp.where(il == m1, sub, SUBLANES), axis=0, keepdims=True)
    il2 = jnp.where(sub == e1, MASK_VALUE, il)
    m2 = jnp.max(il2, axis=0, keepdims=True)
    e2 = jnp.min(jnp.where(il2 == m2, sub, SUBLANES), axis=0, keepdims=True)
    ratio = jnp.exp(m2 - m1)
    w1 = g_w / (1.0 + ratio)
    w2 = w1 * ratio
    erow = lax.broadcasted_iota(jnp.int32, (N_EXPERTS, tm), 0)
    hot1 = erow == g_idx * EXPERTS_PER_GROUP + e1
    hot2 = erow == g_idx * EXPERTS_PER_GROUP + e2

    onehot = jnp.where(hot1, 1.0, jnp.where(hot2, 1.0, 0.0)).astype(BF16)
    rr = lax.broadcasted_iota(jnp.int32, (tm, tm), 0)
    cc = lax.broadcasted_iota(jnp.int32, (tm, tm), 1)
    earlier = jnp.where(rr < cc, 1.0, 0.0).astype(BF16)
    before = jnp.dot(onehot, earlier, preferred_element_type=F32)
    onehot_pad = jnp.concatenate([onehot, jnp.zeros((LANES - N_EXPERTS, tm), BF16)], axis=0)
    cnt = lax.dot_general(jnp.ones((2 * SUBLANES, tm), BF16), onehot_pad, (((1,), (1,)), ((), ())),
                          preferred_element_type=F32)[:SUBLANES]
    nch8 = jnp.floor((cnt + (CHUNK - 1)) * (1.0 / CHUNK))
    r128 = lax.broadcasted_iota(jnp.int32, (LANES, LANES), 0)
    c128 = lax.broadcasted_iota(jnp.int32, (LANES, LANES), 1)
    upper = jnp.where(r128 < c128, 1.0, 0.0).astype(BF16)
    seg_off = CHUNK * jnp.dot(nch8.astype(BF16), upper, preferred_element_type=F32)[0:1, :]
    nch_ref[0] = nch8
    seg_rows = jnp.transpose(jnp.broadcast_to(seg_off, (LANES, LANES)))[:N_EXPERTS]
    pos = before + jnp.concatenate([seg_rows] * (tm // LANES), axis=1)
    pos1 = jnp.sum(jnp.where(hot1, pos, 0.0), axis=0, keepdims=True)
    pos2 = jnp.sum(jnp.where(hot2, pos, 0.0), axis=0, keepdims=True)

    tok8 = jnp.where(sub == 0, pos1, jnp.where(sub == 1, pos2, jnp.where(sub == 2, w1,
                     jnp.where(sub == 3, w2, 0.0))))
    tok_ref[...] = jnp.transpose(
        jnp.concatenate([tok8, jnp.zeros((LANES - SUBLANES, tm), F32)], axis=0))

    p1 = pos1.astype(jnp.int32)
    p2 = pos2.astype(jnp.int32)
    ri = lax.broadcasted_iota(jnp.int32, (xs_ref.shape[0], tm), 0)
    perm = jnp.where(ri == p1, 1.0, jnp.where(ri == p2, 1.0, 0.0)).astype(BF16)
    xs_ref[...] = jnp.dot(perm, h2_hi, preferred_element_type=F32).astype(BF16)


def _out_route(x2, att, lru, wo, g, wr, br, tm):
    n, d = x2.shape
    aw = att.shape[1]
    nblk = n // tm
    rows = _block_rows(tm)
    blk = lambda i: jnp.minimum(i, nblk - 1)
    row = lambda c: pl.BlockSpec((tm, c), lambda i: (blk(i), 0))
    full = lambda shape: pl.BlockSpec(shape, lambda i: (0,) * len(shape))
    return pl.pallas_call(
        _out_route_kernel,
        out_shape=(jax.ShapeDtypeStruct((n, d), F32),
                   jax.ShapeDtypeStruct(((nblk + 1) * rows, d), BF16),
                   jax.ShapeDtypeStruct((n, LANES), F32),
                   jax.ShapeDtypeStruct((nblk, SUBLANES, LANES), F32)),
        grid=(nblk + 1,),
        in_specs=[row(d), row(aw), row(d - aw), full((d, d)), full((1, d)),
                  full((2, d, LANES)), full((1, LANES))],
        out_specs=(row(d), pl.BlockSpec((rows, d), lambda i: (i, 0)), row(LANES),
                   pl.BlockSpec((1, SUBLANES, LANES), lambda i: (blk(i), 0, 0))),
        compiler_params=pltpu.CompilerParams(
            dimension_semantics=("arbitrary",), vmem_limit_bytes=VMEM_LIMIT),
        name="out_route",
    )(x2, att, lru, wo, g, wr, br)


GATHER_AHEAD = 3
X_SLOTS = GATHER_AHEAD + 1
Y_SLOTS = 2


def _experts_kernel(te_ref, nu_ref, src_ref, dst_ref, wnext_ref, wslot_ref,
                    xs_hbm, wg_hbm, wu_hbm, wd_hbm,
                    ys_hbm, xbuf, ybuf, wgbuf, wubuf, wdbuf, wg16, wu16, wd16, gsem, ssem, wsem):
    del xs_hbm
    i = pl.program_id(0)
    nu = nu_ref[0]

    def gather(tile, k):
        row = pl.multiple_of(src_ref[tile * TILE_CHUNKS + k], CHUNK)
        slot = tile % X_SLOTS
        return pltpu.make_async_copy(ys_hbm.at[pl.ds(row, CHUNK), :],
                                     xbuf.at[slot, pl.ds(k * CHUNK, CHUNK), :], gsem.at[slot])

    def scatter(tile, k):
        row = pl.multiple_of(dst_ref[tile * TILE_CHUNKS + k], CHUNK)
        slot = tile % Y_SLOTS
        return pltpu.make_async_copy(ybuf.at[slot, pl.ds(k * CHUNK, CHUNK), :],
                                     ys_hbm.at[pl.ds(row, CHUNK), :], ssem.at[slot])

    def each_chunk(copy, tile, op):
        for k in range(TILE_CHUNKS):
            getattr(copy(tile, k), op)()

    def weight_copies(tile):
        e = te_ref[tile]
        s = wslot_ref[tile]
        return [pltpu.make_async_copy(hbm.at[e], buf.at[s], wsem.at[s])
                for hbm, buf in ((wg_hbm, wgbuf), (wu_hbm, wubuf), (wd_hbm, wdbuf))]

    @pl.when((i == 0) & (nu > 0))
    def _():
        for ahead in range(GATHER_AHEAD):
            each_chunk(gather, ahead, "start")
        for c in weight_copies(0):
            c.start()

    @pl.when((i < nu) & ((i == 0) | (te_ref[i] != te_ref[jnp.maximum(i - 1, 0)])))
    def _():
        for c in weight_copies(i):
            c.wait()
        s = wslot_ref[i]
        wg16[...] = wgbuf[s].astype(BF16)
        wu16[...] = wubuf[s].astype(BF16)
        wd16[...] = wdbuf[s].astype(BF16)
        nxt = wnext_ref[i]

        @pl.when(nxt < nu)
        def _():
            for c in weight_copies(nxt):
                c.start()

    @pl.when(i < nu)
    def _():
        each_chunk(gather, i, "wait")
        x = xbuf[i % X_SLOTS]
        g = jnp.dot(x, wg16[...], preferred_element_type=F32)
        u = jnp.dot(x, wu16[...], preferred_element_type=F32)
        h = (g * _sigmoid(g) * u).astype(BF16)
        ybuf[i % Y_SLOTS] = jnp.dot(h, wd16[...], preferred_element_type=F32).astype(BF16)
        each_chunk(scatter, i, "start")
        each_chunk(gather, i + GATHER_AHEAD, "start")

        @pl.when(i > 0)
        def _():
            each_chunk(scatter, i - 1, "wait")

        @pl.when(i == nu - 1)
        def _():
            for ahead in range(1, GATHER_AHEAD + 1):
                each_chunk(gather, i + ahead, "wait")
            each_chunk(scatter, i, "wait")


W_SLOTS = 2


def _experts(tile_expert, n_used, chunk_src, chunk_dst, w_next, w_slot, xs, wg, wu, wd):
    d = xs.shape[1]
    f = wg.shape[2]
    tm = TILE_CHUNKS * CHUNK
    n_tiles = tile_expert.shape[0]
    n_prefetch = 6
    grid_spec = pltpu.PrefetchScalarGridSpec(
        num_scalar_prefetch=n_prefetch,
        grid=(n_tiles,),
        in_specs=[pl.BlockSpec(memory_space=pl.ANY)] * 4,
        out_specs=pl.BlockSpec(memory_space=pl.ANY),
        scratch_shapes=[pltpu.VMEM((X_SLOTS, tm, d), BF16), pltpu.VMEM((Y_SLOTS, tm, d), BF16),
                        pltpu.VMEM((W_SLOTS, d, f), F32), pltpu.VMEM((W_SLOTS, d, f), F32),
                        pltpu.VMEM((W_SLOTS, f, d), F32),
                        pltpu.VMEM((d, f), BF16), pltpu.VMEM((d, f), BF16), pltpu.VMEM((f, d), BF16),
                        pltpu.SemaphoreType.DMA((X_SLOTS,)), pltpu.SemaphoreType.DMA((Y_SLOTS,)),
                        pltpu.SemaphoreType.DMA((W_SLOTS,))],
    )
    return pl.pallas_call(
        _experts_kernel,
        out_shape=jax.ShapeDtypeStruct(xs.shape, BF16),
        grid_spec=grid_spec,
        input_output_aliases={n_prefetch: 0},
        compiler_params=pltpu.CompilerParams(
            dimension_semantics=("arbitrary",), vmem_limit_bytes=VMEM_LIMIT),
        name="experts",
    )(tile_expert, n_used, chunk_src, chunk_dst, w_next, w_slot, xs, wg, wu, wd)


def _combine_kernel(x1_ref, tok_ref, g_ref, ys_ref, o_ref):
    tm = x1_ref.shape[0]
    rows = ys_ref.shape[0]
    tok = tok_ref[...]
    pos1 = tok[:, 0:1].astype(jnp.int32)
    pos2 = tok[:, 1:2].astype(jnp.int32)
    ci = lax.broadcasted_iota(jnp.int32, (tm, rows), 1)
    unperm = jnp.where(ci == pos1, tok[:, 2:3], jnp.where(ci == pos2, tok[:, 3:4], 0.0))
    y = jnp.dot(unperm.astype(BF16), ys_ref[...], preferred_element_type=F32)
    o_ref[...] = _rms(x1_ref[...] + y, g_ref[...])


def _combine(x1, tok, g, ys, tm):
    n, d = x1.shape
    rows = _block_rows(tm)
    return pl.pallas_call(
        _combine_kernel,
        out_shape=jax.ShapeDtypeStruct((n, d), F32),
        grid=(n // tm,),
        in_specs=[pl.BlockSpec((tm, d), lambda i: (i, 0)),
                  pl.BlockSpec((tm, LANES), lambda i: (i, 0)),
                  pl.BlockSpec((1, d), lambda i: (0, 0)),
                  pl.BlockSpec((rows, d), lambda i: (i, 0))],
        out_specs=pl.BlockSpec((tm, d), lambda i: (i, 0)),
        compiler_params=pltpu.CompilerParams(
            dimension_semantics=("arbitrary",), vmem_limit_bytes=VMEM_LIMIT),
        name="combine",
    )(x1, tok, g, ys)


def _block_diag(w):
    nb, bd, _ = w.shape
    eye = jnp.eye(nb, dtype=w.dtype)
    return (eye[:, None, :, None] * w[:, :, None, :]).reshape(nb * bd, nb * bd)


def _tile_tables(nch, rows):
    nblk = nch.shape[0]
    ex_cum = lambda a, axis: jnp.cumsum(a, axis=axis) - a
    ch_off = ex_cum(nch, 1)
    blk_cum = ex_cum(nch, 0)
    tot = jnp.sum(nch, axis=0)
    tiles_e = (tot + TILE_CHUNKS - 1) // TILE_CHUNKS
    t_end = jnp.cumsum(tiles_e)
    t_start = t_end - tiles_e
    max_chunks = nblk * ((rows - N_EXPERTS * CHUNK) // CHUNK + N_EXPERTS)
    n_tiles = pl.cdiv(max_chunks, TILE_CHUNKS) + N_EXPERTS
    tid = jnp.arange(n_tiles + GATHER_AHEAD, dtype=jnp.int32)
    slot = jnp.arange(TILE_CHUNKS, dtype=jnp.int32)[None, :]
    te = jnp.minimum(jnp.sum((tid[:, None] >= t_end[None, :]).astype(jnp.int32), axis=1),
                     N_EXPERTS - 1)
    is_te = te[:, None] == jnp.arange(N_EXPERTS, dtype=jnp.int32)[None, :]
    of_te = lambda v: jnp.sum(jnp.where(is_te[(...,) + (None,) * (v.ndim - 1)], v[None], 0), axis=1)
    q = (tid - of_te(t_start))[:, None] * TILE_CHUNKS + slot
    valid = (tid < t_end[-1])[:, None] & (q < of_te(tot)[:, None])
    seg_start = of_te(blk_cum.T)
    seg_end = seg_start + of_te(nch.T)
    bidx = jnp.minimum(jnp.sum((q[:, :, None] >= seg_end[:, None, :]).astype(jnp.int32), axis=-1),
                       nblk - 1)
    is_b = bidx[:, :, None] == jnp.arange(nblk, dtype=jnp.int32)[None, None, :]
    of_b = lambda v: jnp.sum(jnp.where(is_b, v[:, None, :], 0), axis=-1)
    row = bidx * rows + CHUNK * (of_b(of_te(ch_off.T)) + q - of_b(seg_start))
    src = jnp.where(valid, row, rows - CHUNK).astype(jnp.int32)
    assert Y_SLOTS * TILE_CHUNKS * CHUNK <= rows
    spare = nblk * rows + CHUNK * ((tid % Y_SLOTS)[:, None] * TILE_CHUNKS + slot)
    dst = jnp.where(valid, row, spare).astype(jnp.int32)
    w_next = of_te(t_end).astype(jnp.int32)
    w_slot = (of_te(jnp.cumsum((tiles_e > 0).astype(jnp.int32))) % W_SLOTS).astype(jnp.int32)
    return (te[:n_tiles], t_end[-1:].astype(jnp.int32), src.reshape(-1),
            dst[:n_tiles].reshape(-1), w_next[:n_tiles], w_slot[:n_tiles])


def kernel(x, mix_norm, w_in, b_forget, conv_w, conv_b, w_a, b_a, w_x, b_x, lru_lambda, w_out,
           ffn_norm, w_group, b_group, w_inner, b_inner, w_gate, w_up, w_down, final_norm):
    b, s, d = x.shape
    assert w_in.shape[0] == 1, "the final rmsnorm is fused into the single layer's combine stage"
    n = b * s
    tm = min(512, s)
    x2 = x.reshape(n, d)

    sizes = (ATT_WIDTH, ATT_WIDTH, ATT_WIDTH, ATT_HEADS, LRU_WIDTH, LRU_WIDTH)
    o = [0]
    for v in sizes:
        o.append(o[-1] + v)
    w_f = jnp.pad(w_in[0][:, o[3]:o[4]], ((0, 0), (0, LANES - ATT_HEADS)))
    w_cat = jnp.concatenate([w_in[0][:, :o[3]], w_in[0][:, o[4]:], w_f], axis=1).astype(BF16)
    bfg = jnp.pad(b_forget[0], (0, LANES - ATT_HEADS)).reshape(1, LANES)
    qkv, lru, cs = _mix_in(x, mix_norm.reshape(1, d), w_cat,
                           conv_w[0], conv_b.reshape(1, -1),
                           _block_diag(w_a[0]).astype(BF16), b_a.reshape(1, -1),
                           _block_diag(w_x[0]).astype(BF16), b_x.reshape(1, -1),
                           lru_lambda.reshape(1, -1), bfg, tm)

    att = _attn(qkv, cs, min(ATT_QUERY_TILE, s))

    gpad = SUBLANES - N_GROUPS
    wr = jnp.concatenate(
        [jnp.pad(w_group[0], ((0, 0), (0, gpad))),
         jnp.transpose(w_inner[0], (1, 0, 2)).reshape(d, N_EXPERTS)], axis=1)
    wr = jnp.pad(wr, ((0, 0), (0, LANES - wr.shape[1])))
    wr_hi = wr.astype(BF16)
    wr = jnp.stack([wr_hi, (wr - wr_hi.astype(F32)).astype(BF16)])
    br = jnp.concatenate([jnp.pad(b_group[0], (0, gpad)), b_inner[0].reshape(-1)])
    br = jnp.pad(br, (0, LANES - br.shape[0])).reshape(1, LANES)
    x1, xs, tok, nch = _out_route(x2, att.reshape(n, -1), lru.reshape(n, -1),
                                  w_out[0].astype(BF16), ffn_norm.reshape(1, d), wr, br, tm)

    rows = _block_rows(tm)
    tables = _tile_tables(nch[:, 0, :N_EXPERTS].astype(jnp.int32), rows)
    fe = w_gate.shape[-1]
    ys = _experts(*tables, xs, w_gate.reshape(N_EXPERTS, d, fe), w_up.reshape(N_EXPERTS, d, fe),
                  w_down.reshape(N_EXPERTS, fe, d))
    out = _combine(x1, tok, final_norm.reshape(1, d), ys, tm)
    return out.reshape(b, s, d)
```
